```python
import jax, jax.numpy as jnp
from jax import lax
import numpy as np

D_MODEL = 2048
BATCH = 32
SEQ = 256
DEPTH = 4
DEC_BATCH = 8
DEC_SEQ = 4096
PAST_LEN = 512

GRID_W = 64
N_HEADS = 16
N_KV_HEADS = 4
HEAD_DIM = 128
GROUP = N_HEADS // N_KV_HEADS
BRANCH = N_HEADS * HEAD_DIM
KV_W = N_KV_HEADS * HEAD_DIM
PROJ_W = 2 * BRANCH + 2 * KV_W
N_MIXERS = 2
N_WINDOW_LAYERS = (DEPTH + 1) // 2
WINDOW = 128
Q_BLOCK = 128
ROPE_THETA = 10000.0
EPS = 1e-6
NEG_INF = -1e30

kernel_name = "hybrid_dit_prefix_window_axial_gqa_step"


def rmsnorm(x, g):
    xf = x.astype(jnp.float32)
    y = xf * lax.rsqrt(jnp.mean(xf * xf, axis=-1, keepdims=True) + EPS)
    return (y * g.astype(jnp.float32)).astype(x.dtype)


def axial_rope_tables(n_tokens):
    rows = n_tokens // GRID_W
    row = jnp.repeat(jnp.arange(rows), GRID_W).astype(jnp.float32)
    col = jnp.tile(jnp.arange(GRID_W), rows).astype(jnp.float32)
    n_freq = HEAD_DIM // 4
    inv = ROPE_THETA ** (-jnp.arange(n_freq, dtype=jnp.float32) / n_freq)
    ang = jnp.stack([row[:, None] * inv, col[:, None] * inv], axis=1)
    return jnp.cos(ang), jnp.sin(ang)


def apply_axial_rope(x, cos, sin):
    b, t, h, d = x.shape
    xf = x.astype(jnp.float32).reshape(b, t, h, 2, 2, d // 4)
    x1, x2 = xf[..., 0, :], xf[..., 1, :]
    cs, sn = cos[:, None], sin[:, None]
    out = jnp.stack([x1 * cs - x2 * sn, x2 * cs + x1 * sn], axis=-2)
    return out.reshape(b, t, h, d).astype(x.dtype)


def attend(q, k, v, mask, sink):
    s = jnp.einsum('bqhgd,bkhd->bhgqk', q, k, preferred_element_type=jnp.float32) * (HEAD_DIM ** -0.5)
    if mask is not None:
        s = jnp.where(mask, s, NEG_INF)
    if sink is not None:
        sk = jnp.broadcast_to(sink.astype(jnp.float32)[None, :, :, None, None], s.shape[:-1] + (1,))
        p = jax.nn.softmax(jnp.concatenate([s, sk], axis=-1), axis=-1)[..., :-1]
    else:
        p = jax.nn.softmax(s, axis=-1)
    return jnp.einsum('bhgqk,bkhd->bqhgd', p.astype(v.dtype), v)


def to_blocks(q):
    b, t = q.shape[:2]
    return q.reshape(b, t // Q_BLOCK, Q_BLOCK, *q.shape[2:]).swapaxes(0, 1)


def from_blocks(o):
    nb, b = o.shape[:2]
    return o.swapaxes(0, 1).reshape(b, nb * Q_BLOCK, BRANCH)


def dense_attention(q, k, v, sink):
    o = lax.map(lambda qb: attend(qb, k, v, None, sink), to_blocks(q))
    return from_blocks(o)


def window_attention(q, k, v, k_ctx, v_ctx, sink):
    t = q.shape[1]
    nb = t // Q_BLOCK
    pad = ((0, 0), (WINDOW, WINDOW), (0, 0), (0, 0))
    k_pad, v_pad = jnp.pad(k, pad), jnp.pad(v, pad)
    span = Q_BLOCK + 2 * WINDOW
    i = jnp.arange(Q_BLOCK)[:, None]
    j = jnp.arange(span)[None, :]
    band = (j - i >= 0) & (j - i <= 2 * WINDOW)
    ctx_ok = jnp.ones((Q_BLOCK, k_ctx.shape[1]), dtype=bool)

    def block(args):
        qb, blk = args
        start = blk * Q_BLOCK
        kb = lax.dynamic_slice_in_dim(k_pad, start, span, axis=1)
        vb = lax.dynamic_slice_in_dim(v_pad, start, span, axis=1)
        pos = start - WINDOW + j
        mask = jnp.concatenate([band & (pos >= 0) & (pos < t), ctx_ok], axis=1)
        return attend(qb, jnp.concatenate([kb, k_ctx], axis=1),
                      jnp.concatenate([vb, v_ctx], axis=1), mask, sink)

    o = lax.map(block, (to_blocks(q), jnp.arange(nb)))
    return from_blocks(o)


def branch_in(x, mod, norm_g, w_in, q_g, k_g):
    shift, scale, gate = jnp.split(mod, 3, axis=-1)
    h = rmsnorm(x, norm_g) * (1 + scale[:, None]) + shift[:, None]
    proj = h @ w_in
    q, z, k, v = jnp.split(proj, [BRANCH, 2 * BRANCH, 2 * BRANCH + KV_W], axis=-1)
    b, t = x.shape[:2]
    q = rmsnorm(q.reshape(b, t, N_HEADS, HEAD_DIM), q_g)
    k = rmsnorm(k.reshape(b, t, N_KV_HEADS, HEAD_DIM), k_g)
    v = v.reshape(b, t, N_KV_HEADS, HEAD_DIM)
    return q, z, k, v, gate


def group_q(q):
    b, t = q.shape[:2]
    return q.reshape(b, t, N_KV_HEADS, GROUP, HEAD_DIM)


def branch_out(x, attn, z, gate, w_out):
    y = (attn * jax.nn.silu(z)) @ w_out
    return x + gate[:, None] * y


def setup_inputs(seed: int = 0) -> dict:
    key = jax.random.key(seed)
    ks = jax.random.split(key, 14)
    f32 = jnp.float32
    cache_shape = (DEC_BATCH, DEPTH, PAST_LEN, N_KV_HEADS, HEAD_DIM)
    return {
        "x_prompt": jax.random.normal(ks[0], (BATCH, SEQ, D_MODEL), f32),
        "x_sample": jax.random.normal(ks[1], (DEC_BATCH, DEC_SEQ, D_MODEL), f32),
        "c": jax.random.normal(ks[2], (DEC_BATCH, D_MODEL), f32),
        "cache_k": jax.random.normal(ks[3], cache_shape, f32),
        "cache_v": jax.random.normal(ks[4], cache_shape, f32),
        "c_ctx": jax.random.normal(ks[5], (D_MODEL,), f32),
        "norm_g": 1.0 + 0.02 * jax.random.normal(ks[6], (DEPTH, D_MODEL), f32),
        "w_mod": 0.5 * D_MODEL ** -0.5 * jax.random.normal(ks[7], (DEPTH, D_MODEL, 3 * D_MODEL), f32),
        "b_mod": 0.02 * jax.random.normal(ks[8], (DEPTH, 3 * D_MODEL), f32),
        "w_in": D_MODEL ** -0.5 * jax.random.normal(ks[9], (DEPTH, D_MODEL, PROJ_W), f32),
        "q_norm_g": 1.0 + 0.02 * jax.random.normal(ks[10], (DEPTH, HEAD_DIM), f32),
        "k_norm_g": 1.0 + 0.02 * jax.random.normal(ks[11], (DEPTH, HEAD_DIM), f32),
        "sinks": jax.random.normal(ks[12], (N_WINDOW_LAYERS, N_HEADS), f32),
        "w_out": BRANCH ** -0.5 * jax.random.normal(ks[13], (DEPTH, BRANCH, D_MODEL), f32),
    }


def reference(x_prompt, x_sample, c, cache_k, cache_v, c_ctx, norm_g, w_mod, b_mod, w_in,
              q_norm_g, k_norm_g, sinks, w_out):
    rope = axial_rope_tables(x_sample.shape[1])
    xp, xs = x_prompt, x_sample
    new_k, new_v = [], []
    for l in range(DEPTH):
        use_window = (l % N_MIXERS) == 0
        sink = sinks[l // N_MIXERS].reshape(N_KV_HEADS, GROUP) if use_window else None

        mod_c = (jax.nn.silu(c_ctx) @ w_mod[l] + b_mod[l])[None]
        q, z, k, v, gate = branch_in(xp, mod_c, norm_g[l], w_in[l], q_norm_g[l], k_norm_g[l])
        attn = dense_attention(group_q(q), k, v, sink)
        xp = branch_out(xp, attn, z, gate, w_out[l])
        new_k.append(k)
        new_v.append(v)

        mod_s = jax.nn.silu(c) @ w_mod[l] + b_mod[l]
        q, z, k, v, gate = branch_in(xs, mod_s, norm_g[l], w_in[l], q_norm_g[l], k_norm_g[l])
        q = group_q(apply_axial_rope(q, *rope))
        k = apply_axial_rope(k, *rope)
        k_ctx, v_ctx = cache_k[:, l], cache_v[:, l]
        if use_window:
            attn = window_attention(q, k, v, k_ctx, v_ctx, sink)
        else:
            attn = dense_attention(q, jnp.concatenate([k, k_ctx], axis=1),
                                   jnp.concatenate([v, v_ctx], axis=1), None)
        xs = branch_out(xs, attn, z, gate, w_out[l])

    new_cache_k = jnp.stack(new_k, axis=1)
    new_cache_v = jnp.stack(new_v, axis=1)
    return (xp, xs, new_cache_k, new_cache_v)
```

```python
import functools

import jax
import jax.numpy as jnp
from jax import lax
from jax.experimental import pallas as pl
from jax.experimental.pallas import tpu as pltpu

GRID_W = 64
WINDOW = 128
Q_BLOCK = 128
ROPE_THETA = 10000.0
EPS = 1e-6
NEG_INF = -1e30

VMEM_LIMIT_BYTES = 56 * 1024 * 1024
MOD_ROWS = 16

bf16 = jnp.bfloat16
f32 = jnp.float32

_NT = (((1,), (1,)), ((), ()))


def _params(n_axes):
    return pltpu.CompilerParams(
        dimension_semantics=("arbitrary",) * n_axes, vmem_limit_bytes=VMEM_LIMIT_BYTES)


def _mod_kernel(cc_ref, w_ref, b_ref, o_ref):
    cc = cc_ref[...]
    a = (cc * jax.nn.sigmoid(cc)).astype(bf16)
    o_ref[...] = jnp.dot(a, w_ref[...].astype(bf16), preferred_element_type=f32) + b_ref[...]


def _modulation(cc, w_mod, b_mod):
    depth, d, n = w_mod.shape
    tn = 1024 if n % 1024 == 0 else n
    return pl.pallas_call(
        _mod_kernel,
        grid=(depth, n // tn),
        in_specs=[
            pl.BlockSpec((MOD_ROWS, d), lambda l, j: (0, 0)),
            pl.BlockSpec((None, d, tn), lambda l, j: (l, 0, j)),
            pl.BlockSpec((None, 1, tn), lambda l, j: (l, 0, j)),
        ],
        out_specs=pl.BlockSpec((None, MOD_ROWS, tn), lambda l, j: (l, 0, j)),
        out_shape=jax.ShapeDtypeStruct((depth, MOD_ROWS, n), f32),
        compiler_params=_params(2),
        name="modulation",
    )(cc, w_mod, b_mod.reshape(depth, 1, n))


def _head_rms(t, g):
    ms = jnp.mean(t * t, axis=-1, keepdims=True)
    return t * lax.rsqrt(ms + EPS) * g


def _in_proj_kernel(*refs, branch, kv_w, dh, chunk, rope, cache_out, q_scale):
    x_ref, shift_ref, scale_ref, ng_ref, w_ref, qg_ref, kg_ref = refs[:7]
    refs = refs[7:]
    if rope:
        cos_ref, sin_ref = refs[:2]
        refs = refs[2:]
    q_ref, sz_ref, k_ref, v_ref = refs[:4]
    if cache_out:
        kf_ref, vf_ref = refs[4:6]

    x = x_ref[...]
    ms = jnp.mean(x * x, axis=-1, keepdims=True)
    h = x * lax.rsqrt(ms + EPS) * ng_ref[...]
    h = h * (1.0 + scale_ref[...]) + shift_ref[...]
    hb = h.astype(bf16)

    if rope:
        cos = cos_ref[...]
        sin = sin_ref[...]
        lane = lax.broadcasted_iota(jnp.int32, cos.shape, 1)
        first_half = (lane // (dh // 4)) % 2 == 0

        def rotary(t):
            partner = jnp.where(first_half, pltpu.roll(t, dh - dh // 4, 1), pltpu.roll(t, dh // 4, 1))
            return t * cos + partner * sin
    else:
        def rotary(t):
            return t

    qg = qg_ref[...]
    kg = kg_ref[...]
    heads_per_chunk = chunk // dh
    for c in range((2 * branch + 2 * kv_w) // chunk):
        col = c * chunk
        r = jnp.dot(hb, w_ref[:, col:col + chunk], preferred_element_type=f32)
        if col < branch:
            for j in range(heads_per_chunk):
                t = rotary(_head_rms(r[:, j * dh:(j + 1) * dh], qg)) * q_scale
                q_ref[:, col + j * dh:col + (j + 1) * dh] = t.astype(bf16)
        elif col < 2 * branch:
            sz_ref[:, col - branch:col - branch + chunk] = (r * jax.nn.sigmoid(r)).astype(bf16)
        elif col < 2 * branch + kv_w:
            base = col - 2 * branch
            for j in range(heads_per_chunk):
                t = _head_rms(r[:, j * dh:(j + 1) * dh], kg)
                if cache_out:
                    kf_ref[:, base + j * dh:base + (j + 1) * dh] = t
                k_ref[:, base + j * dh:base + (j + 1) * dh] = rotary(t).astype(bf16)
        else:
            base = col - 2 * branch - kv_w
            if cache_out:
                vf_ref[:, base:base + chunk] = r
            v_ref[:, base:base + chunk] = r.astype(bf16)


def _in_proj(x, mod4, norm_g3, w_in, qg3, kg3, layer, *, tm, rows_per_mod, mod_row0, branch, kv_w, dh,
             rope_tables=None, cache_out=False):
    t, d = x.shape
    proj_w = w_in.shape[-1]
    chunk = min(512, kv_w)
    rope = rope_tables is not None

    def mod_row(i):
        return mod_row0 + (i * tm) // rows_per_mod if rows_per_mod else mod_row0

    in_specs = [
        pl.BlockSpec((tm, d), lambda i: (i, 0)),
        pl.BlockSpec((None, None, 1, d), lambda i: (layer, mod_row(i), 0, 0)),
        pl.BlockSpec((None, None, 1, d), lambda i: (layer, mod_row(i), 0, 1)),
        pl.BlockSpec((None, 1, d), lambda i: (layer, 0, 0)),
        pl.BlockSpec((None, d, proj_w), lambda i: (layer, 0, 0), pipeline_mode=pl.Buffered(1)),
        pl.BlockSpec((None, 1, dh), lambda i: (layer, 0, 0)),
        pl.BlockSpec((None, 1, dh), lambda i: (layer, 0, 0)),
    ]
    args = [x, mod4, mod4, norm_g3, w_in, qg3, kg3]
    if rope:
        seq = rope_tables[0].shape[0]
        nb = seq // tm
        in_specs += [pl.BlockSpec((tm, dh), lambda i: (i % nb, 0))] * 2
        args += list(rope_tables)
    out_shape = [jax.ShapeDtypeStruct((t, branch), bf16), jax.ShapeDtypeStruct((t, branch), bf16),
                 jax.ShapeDtypeStruct((t, kv_w), bf16), jax.ShapeDtypeStruct((t, kv_w), bf16)]
    out_specs = [pl.BlockSpec((tm, branch), lambda i: (i, 0)), pl.BlockSpec((tm, branch), lambda i: (i, 0)),
                 pl.BlockSpec((tm, kv_w), lambda i: (i, 0)), pl.BlockSpec((tm, kv_w), lambda i: (i, 0))]
    if cache_out:
        out_shape += [jax.ShapeDtypeStruct((t, kv_w), f32)] * 2
        out_specs += [pl.BlockSpec((tm, kv_w), lambda i: (i, 0))] * 2
    kern = functools.partial(_in_proj_kernel, branch=branch, kv_w=kv_w, dh=dh, chunk=chunk, rope=rope,
                             cache_out=cache_out, q_scale=dh ** -0.5)
    return pl.pallas_call(
        kern, grid=(t // tm,), in_specs=in_specs, out_specs=out_specs, out_shape=out_shape,
        compiler_params=_params(1), name="in_proj_latent" if rope else "in_proj_context",
    )(*args)


def _stack_groups(qb, groups, dh):
    return jnp.concatenate([qb[:, g * dh:(g + 1) * dh] for g in range(groups)], axis=0)


def _sink_column(sink_ref, head0, groups, rows):
    rowg = lax.broadcasted_iota(jnp.int32, (groups * rows, 1), 0) // rows
    col = jnp.zeros((groups * rows, 1), f32)
    for g in range(groups):
        col = jnp.where(rowg == g, sink_ref[head0 + g], col)
    return col


def _store_gated(o_ref, row0, rows, o, szb, groups, dh):
    for g in range(groups):
        u = o[g * rows:(g + 1) * rows] * szb[:, g * dh:(g + 1) * dh].astype(f32)
        o_ref[pl.ds(row0, rows), g * dh:(g + 1) * dh] = u.astype(bf16)


def _attn_context_kernel(*refs, groups, dh, use_sink):
    if use_sink:
        sink_ref, q_ref, sz_ref, k_ref, v_ref, o_ref = refs
    else:
        q_ref, sz_ref, k_ref, v_ref, o_ref = refs
    rows = q_ref.shape[0]
    qs = _stack_groups(q_ref[...], groups, dh)
    s = lax.dot_general(qs, k_ref[...], _NT, preferred_element_type=f32)
    m = jnp.max(s, axis=-1, keepdims=True)
    if use_sink:
        sink = _sink_column(sink_ref, pl.program_id(1) * groups, groups, rows)
        m = jnp.maximum(m, sink)
    p = jnp.exp(s - m)
    l = jnp.sum(p, axis=-1, keepdims=True)
    if use_sink:
        l = l + jnp.exp(sink - m)
    o = jnp.dot(p.astype(bf16), v_ref[...], preferred_element_type=f32) / l
    _store_gated(o_ref, 0, rows, o, sz_ref[...], groups, dh)


def _attn_context(q, sz, k, v, sink, *, batch, seq, groups, dh):
    t, branch = q.shape
    n_kv = k.shape[1] // dh
    gw = groups * dh
    use_sink = sink is not None
    in_specs = [
        pl.BlockSpec((seq, gw), lambda b, h: (b, h)),
        pl.BlockSpec((seq, gw), lambda b, h: (b, h)),
        pl.BlockSpec((seq, dh), lambda b, h: (b, h)),
        pl.BlockSpec((seq, dh), lambda b, h: (b, h)),
    ]
    args = [q, sz, k, v]
    if use_sink:
        in_specs = [pl.BlockSpec(memory_space=pltpu.SMEM)] + in_specs
        args = [sink] + args
    return pl.pallas_call(
        functools.partial(_attn_context_kernel, groups=groups, dh=dh, use_sink=use_sink),
        grid=(batch, n_kv), in_specs=in_specs,
        out_specs=pl.BlockSpec((seq, gw), lambda b, h: (b, h)),
        out_shape=jax.ShapeDtypeStruct((t, branch), bf16),
        compiler_params=_params(2), name="attn_context",
    )(*args)


def _attn_window_kernel(sink_ref, q_ref, sz_ref, k_ref, v_ref, kc_ref, vc_ref, o_ref, *, groups, dh, tq, window):
    seq = q_ref.shape[0]
    span = tq + 2 * window
    kc = kc_ref[...].astype(bf16)
    vc = vc_ref[...].astype(bf16)
    sink = _sink_column(sink_ref, pl.program_id(1) * groups, groups, tq)
    row_tok = lax.broadcasted_iota(jnp.int32, (groups * tq, span), 0) % tq
    rel = row_tok - lax.broadcasted_iota(jnp.int32, (groups * tq, span), 1)

    def body(i, carry):
        q0 = pl.multiple_of(i * tq, tq)
        k0 = pl.multiple_of(jnp.clip(q0 - window, 0, seq - span), tq)
        qs = _stack_groups(q_ref[pl.ds(q0, tq), :], groups, dh)
        s1 = lax.dot_general(qs, k_ref[pl.ds(k0, span), :], _NT, preferred_element_type=f32)
        dist = rel + (q0 - k0)
        s1 = jnp.where((dist >= -window) & (dist <= window), s1, NEG_INF)
        s2 = lax.dot_general(qs, kc, _NT, preferred_element_type=f32)
        m = jnp.maximum(jnp.maximum(jnp.max(s1, axis=-1, keepdims=True),
                                    jnp.max(s2, axis=-1, keepdims=True)), sink)
        p1 = jnp.exp(s1 - m)
        p2 = jnp.exp(s2 - m)
        l = jnp.sum(p1, axis=-1, keepdims=True) + jnp.sum(p2, axis=-1, keepdims=True) + jnp.exp(sink - m)
        o = (jnp.dot(p1.astype(bf16), v_ref[pl.ds(k0, span), :], preferred_element_type=f32)
             + jnp.dot(p2.astype(bf16), vc, preferred_element_type=f32)) / l
        _store_gated(o_ref, q0, tq, o, sz_ref[pl.ds(q0, tq), :], groups, dh)
        return carry

    lax.fori_loop(0, seq // tq, body, 0)


def _attn_window(q, sz, k, v, cache_k4, cache_v4, sink, layer, *, batch, seq, groups, dh):
    t, branch = q.shape
    n_kv = k.shape[1] // dh
    gw = groups * dh
    past = cache_k4.shape[2]
    assert seq >= Q_BLOCK + 2 * WINDOW and seq % Q_BLOCK == 0 and WINDOW % Q_BLOCK == 0
    ctx_spec = pl.BlockSpec((None, None, past, dh), lambda b, h: (b, layer, 0, h))
    return pl.pallas_call(
        functools.partial(_attn_window_kernel, groups=groups, dh=dh, tq=Q_BLOCK, window=WINDOW),
        grid=(batch, n_kv),
        in_specs=[
            pl.BlockSpec(memory_space=pltpu.SMEM),
            pl.BlockSpec((seq, gw), lambda b, h: (b, h)),
            pl.BlockSpec((seq, gw), lambda b, h: (b, h)),
            pl.BlockSpec((seq, dh), lambda b, h: (b, h)),
            pl.BlockSpec((seq, dh), lambda b, h: (b, h)),
            ctx_spec, ctx_spec,
        ],
        out_specs=pl.BlockSpec((seq, gw), lambda b, h: (b, h)),
        out_shape=jax.ShapeDtypeStruct((t, branch), bf16),
        compiler_params=_params(2), name="attn_window",
    )(sink, q, sz, k, v, cache_k4, cache_v4)


def _attn_dense_kernel(q_ref, sz_ref, k_ref, v_ref, kc_ref, vc_ref, o_ref, m_ref, l_ref, acc_ref, *,
                       groups, dh, tk):
    tq = q_ref.shape[0]
    seq = k_ref.shape[0]
    qs = _stack_groups(q_ref[...], groups, dh)
    m_ref[...] = jnp.full(m_ref.shape, NEG_INF, f32)
    l_ref[...] = jnp.zeros(l_ref.shape, f32)
    acc_ref[...] = jnp.zeros(acc_ref.shape, f32)

    def step(kb, vb):
        s = lax.dot_general(qs, kb, _NT, preferred_element_type=f32)
        m_old = m_ref[...]
        m_new = jnp.maximum(m_old, jnp.max(s, axis=-1, keepdims=True))
        alpha = jnp.exp(m_old - m_new)
        p = jnp.exp(s - m_new)
        l_ref[...] = alpha * l_ref[...] + jnp.sum(p, axis=-1, keepdims=True)
        acc_ref[...] = alpha * acc_ref[...] + jnp.dot(p.astype(bf16), vb, preferred_element_type=f32)
        m_ref[...] = m_new

    def body(j, carry):
        k0 = pl.multiple_of(j * tk, tk)
        step(k_ref[pl.ds(k0, tk), :], v_ref[pl.ds(k0, tk), :])
        return carry

    lax.fori_loop(0, seq // tk, body, 0)
    step(kc_ref[...].astype(bf16), vc_ref[...].astype(bf16))
    o = acc_ref[...] / l_ref[...]
    _store_gated(o_ref, 0, tq, o, sz_ref[...], groups, dh)


def _attn_dense(q, sz, k, v, cache_k4, cache_v4, layer, *, batch, seq, groups, dh, tq, tk):
    t, branch = q.shape
    n_kv = k.shape[1] // dh
    gw = groups * dh
    past = cache_k4.shape[2]
    nq = seq // tq
    ctx_spec = pl.BlockSpec((None, None, past, dh), lambda b, h, i: (b, layer, 0, h))
    return pl.pallas_call(
        functools.partial(_attn_dense_kernel, groups=groups, dh=dh, tk=tk),
        grid=(batch, n_kv, nq),
        in_specs=[
            pl.BlockSpec((tq, gw), lambda b, h, i: (b * nq + i, h)),
            pl.BlockSpec((tq, gw), lambda b, h, i: (b * nq + i, h)),
            pl.BlockSpec((seq, dh), lambda b, h, i: (b, h)),
            pl.BlockSpec((seq, dh), lambda b, h, i: (b, h)),
            ctx_spec, ctx_spec,
        ],
        out_specs=pl.BlockSpec((tq, gw), lambda b, h, i: (b * nq + i, h)),
        out_shape=jax.ShapeDtypeStruct((t, branch), bf16),
        scratch_shapes=[pltpu.VMEM((groups * tq, 1), f32), pltpu.VMEM((groups * tq, 1), f32),
                        pltpu.VMEM((groups * tq, dh), f32)],
        compiler_params=_params(3), name="attn_dense",
    )(q, sz, k, v, cache_k4, cache_v4)


def _out_proj_kernel(u_ref, x_ref, gate_ref, w_ref, o_ref):
    y = jnp.dot(u_ref[...], w_ref[...], preferred_element_type=f32)
    o_ref[...] = x_ref[...] + gate_ref[...] * y


def _out_proj(u, x, mod4, w_out, layer, *, tm, rows_per_mod, mod_row0):
    t, d = x.shape
    branch = u.shape[1]

    def mod_row(i):
        return mod_row0 + (i * tm) // rows_per_mod if rows_per_mod else mod_row0

    return pl.pallas_call(
        _out_proj_kernel, grid=(t // tm,),
        in_specs=[
            pl.BlockSpec((tm, branch), lambda i: (i, 0)),
            pl.BlockSpec((tm, d), lambda i: (i, 0)),
            pl.BlockSpec((None, None, 1, d), lambda i: (layer, mod_row(i), 0, 2)),
            pl.BlockSpec((None, branch, d), lambda i: (layer, 0, 0), pipeline_mode=pl.Buffered(1)),
        ],
        out_specs=pl.BlockSpec((tm, d), lambda i: (i, 0)),
        out_shape=jax.ShapeDtypeStruct((t, d), f32),
        compiler_params=_params(1), name="out_proj",
    )(u, x, mod4, w_out)


def _rope_tables(n_tokens, dh):
    rows = n_tokens // GRID_W
    row = jnp.repeat(jnp.arange(rows), GRID_W).astype(f32)
    col = jnp.tile(jnp.arange(GRID_W), rows).astype(f32)
    n_freq = dh // 4
    inv = ROPE_THETA ** (-jnp.arange(n_freq, dtype=f32) / n_freq)
    ar, ac = row[:, None] * inv, col[:, None] * inv
    cos = jnp.concatenate([jnp.cos(ar), jnp.cos(ar), jnp.cos(ac), jnp.cos(ac)], axis=1)
    sin = jnp.concatenate([-jnp.sin(ar), jnp.sin(ar), -jnp.sin(ac), jnp.sin(ac)], axis=1)
    return cos, sin


def kernel(x_prompt, x_sample, c, cache_k, cache_v, c_ctx, norm_g, w_mod, b_mod, w_in,
           q_norm_g, k_norm_g, sinks, w_out):
    batch, seq, d = x_prompt.shape
    dec_batch, dec_seq, _ = x_sample.shape
    depth = w_in.shape[0]
    past, n_kv, dh = cache_k.shape[2:]
    n_heads = sinks.shape[1]
    groups = n_heads // n_kv
    branch = n_heads * dh
    kv_w = n_kv * dh
    assert dec_batch + 1 <= MOD_ROWS

    cc = jnp.concatenate([c_ctx[None], c, jnp.zeros((MOD_ROWS - 1 - dec_batch, d), f32)], axis=0)
    mod4 = _modulation(cc, w_mod, b_mod).reshape(depth, MOD_ROWS, 1, 3 * d)

    w_in_b = w_in.astype(bf16)
    w_out_b = w_out.astype(bf16)
    norm_g3 = norm_g.reshape(depth, 1, d)
    qg3 = q_norm_g.reshape(depth, 1, dh)
    kg3 = k_norm_g.reshape(depth, 1, dh)
    cache_k4 = cache_k.reshape(dec_batch, depth, past, kv_w)
    cache_v4 = cache_v.reshape(dec_batch, depth, past, kv_w)
    rope = _rope_tables(dec_seq, dh)

    xp = x_prompt.reshape(batch * seq, d)
    xs = x_sample.reshape(dec_batch * dec_seq, d)
    tm = 256
    dims = dict(branch=branch, kv_w=kv_w, dh=dh)
    new_k, new_v = [], []
    for layer in range(depth):
        use_window = layer % 2 == 0
        sink = sinks[layer // 2] if use_window else None

        q, sz, k, v, kf, vf = _in_proj(xp, mod4, norm_g3, w_in_b, qg3, kg3, layer, tm=tm, rows_per_mod=0,
                                       mod_row0=0, cache_out=True, **dims)
        u = _attn_context(q, sz, k, v, sink, batch=batch, seq=seq, groups=groups, dh=dh)
        xp = _out_proj(u, xp, mod4, w_out_b, layer, tm=tm, rows_per_mod=0, mod_row0=0)
        new_k.append(kf.reshape(batch, seq, n_kv, dh))
        new_v.append(vf.reshape(batch, seq, n_kv, dh))

        q, sz, k, v = _in_proj(xs, mod4, norm_g3, w_in_b, qg3, kg3, layer, tm=tm, rows_per_mod=dec_seq,
                               mod_row0=1, rope_tables=rope, **dims)
        if use_window:
            u = _attn_window(q, sz, k, v, cache_k4, cache_v4, sink, layer, batch=dec_batch, seq=dec_seq,
                             groups=groups, dh=dh)
        else:
            u = _attn_dense(q, sz, k, v, cache_k4, cache_v4, layer, batch=dec_batch, seq=dec_seq,
                            groups=groups, dh=dh, tq=256, tk=512)
        xs = _out_proj(u, xs, mod4, w_out_b, layer, tm=tm, rows_per_mod=dec_seq, mod_row0=1)

    return (xp.reshape(batch, seq, d), xs.reshape(dec_batch, dec_seq, d),
            jnp.stack(new_k, axis=1), jnp.stack(new_v, axis=1))
```

```python
import functools

import jax
import jax.numpy as jnp
from jax import lax
from jax.experimental import pallas as pl
from jax.experimental.pallas import tpu as pltpu

GRID_W = 64
WINDOW = 128
Q_BLOCK = 128
ROPE_THETA = 10000.0
EPS = 1e-6
NEG_INF = -1e30
LOG2_E = 1.4426950408889634

VMEM_LIMIT_BYTES = 56 * 1024 * 1024
MOD_ROWS = 16

bf16 = jnp.bfloat16
f32 = jnp.float32

_NT = (((1,), (1,)), ((), ()))


def _params(n_axes):
    return pltpu.CompilerParams(
        dimension_semantics=("arbitrary",) * n_axes, vmem_limit_bytes=VMEM_LIMIT_BYTES)


def _mod_kernel(cc_ref, w_ref, b_ref, o_ref):
    cc = cc_ref[...]
    a = (cc * jax.nn.sigmoid(cc)).astype(bf16)
    o_ref[...] = jnp.dot(a, w_ref[...].astype(bf16), preferred_element_type=f32) + b_ref[...]


def _modulation(cc, w_mod, b_mod):
    depth, d, n = w_mod.shape
    tn = 1024 if n % 1024 == 0 else n
    return pl.pallas_call(
        _mod_kernel,
        grid=(depth, n // tn),
        in_specs=[
            pl.BlockSpec((MOD_ROWS, d), lambda l, j: (0, 0)),
            pl.BlockSpec((None, d, tn), lambda l, j: (l, 0, j)),
            pl.BlockSpec((None, 1, tn), lambda l, j: (l, 0, j)),
        ],
        out_specs=pl.BlockSpec((None, MOD_ROWS, tn), lambda l, j: (l, 0, j)),
        out_shape=jax.ShapeDtypeStruct((depth, MOD_ROWS, n), f32),
        compiler_params=_params(2),
        name="modulation",
    )(cc, w_mod, b_mod.reshape(depth, 1, n))


def _head_rms(t, g):
    ms = jnp.mean(t * t, axis=-1, keepdims=True)
    return t * lax.rsqrt(ms + EPS) * g


def _in_proj_kernel(*refs, branch, kv_w, dh, chunk, rope, cache_out, q_scale):
    x_ref, shift_ref, scale_ref, ng_ref, w_ref, qg_ref, kg_ref = refs[:7]
    refs = refs[7:]
    if rope:
        cos_ref, sin_ref = refs[:2]
        refs = refs[2:]
    q_ref, sz_ref, k_ref, v_ref = refs[:4]
    if cache_out:
        kf_ref, vf_ref = refs[4:6]

    x = x_ref[...]
    ms = jnp.mean(x * x, axis=-1, keepdims=True)
    h = x * lax.rsqrt(ms + EPS) * ng_ref[...]
    h = h * (1.0 + scale_ref[...]) + shift_ref[...]
    hb = h.astype(bf16)

    if rope:
        cos = cos_ref[...]
        sin = sin_ref[...]
        lane = lax.broadcasted_iota(jnp.int32, cos.shape, 1)
        first_half = (lane // (dh // 4)) % 2 == 0

        def rotary(t):
            partner = jnp.where(first_half, pltpu.roll(t, dh - dh // 4, 1), pltpu.roll(t, dh // 4, 1))
            return t * cos + partner * sin
    else:
        def rotary(t):
            return t

    qg = qg_ref[...]
    kg = kg_ref[...]
    heads_per_chunk = chunk // dh
    for c in range((2 * branch + 2 * kv_w) // chunk):
        col = c * chunk
        r = jnp.dot(hb, w_ref[:, col:col + chunk], preferred_element_type=f32)
        if col < branch:
            for j in range(heads_per_chunk):
                t = rotary(_head_rms(r[:, j * dh:(j + 1) * dh], qg)) * q_scale
                q_ref[:, col + j * dh:col + (j + 1) * dh] = t.astype(bf16)
        elif col < 2 * branch:
            sz_ref[:, col - branch:col - branch + chunk] = (r * jax.nn.sigmoid(r)).astype(bf16)
        elif col < 2 * branch + kv_w:
            base = col - 2 * branch
            for j in range(heads_per_chunk):
                t = _head_rms(r[:, j * dh:(j + 1) * dh], kg)
                if cache_out:
                    kf_ref[:, base + j * dh:base + (j + 1) * dh] = t
                k_ref[:, base + j * dh:base + (j + 1) * dh] = rotary(t).astype(bf16)
        else:
            base = col - 2 * branch - kv_w
            if cache_out:
                vf_ref[:, base:base + chunk] = r
            v_ref[:, base:base + chunk] = r.astype(bf16)


def _in_proj(x, mod4, norm_g3, w_in, qg3, kg3, layer, *, tm, rows_per_mod, mod_row0, branch, kv_w, dh,
             rope_tables=None, cache_out=False, q_scale=None):
    t, d = x.shape
    proj_w = w_in.shape[-1]
    chunk = min(512, kv_w)
    rope = rope_tables is not None

    def mod_row(i):
        return mod_row0 + (i * tm) // rows_per_mod if rows_per_mod else mod_row0

    in_specs = [
        pl.BlockSpec((tm, d), lambda i: (i, 0)),
        pl.BlockSpec((None, None, 1, d), lambda i: (layer, mod_row(i), 0, 0)),
        pl.BlockSpec((None, None, 1, d), lambda i: (layer, mod_row(i), 0, 1)),
        pl.BlockSpec((None, 1, d), lambda i: (layer, 0, 0)),
        pl.BlockSpec((None, d, proj_w), lambda i: (layer, 0, 0), pipeline_mode=pl.Buffered(1)),
        pl.BlockSpec((None, 1, dh), lambda i: (layer, 0, 0)),
        pl.BlockSpec((None, 1, dh), lambda i: (layer, 0, 0)),
    ]
    args = [x, mod4, mod4, norm_g3, w_in, qg3, kg3]
    if rope:
        seq = rope_tables[0].shape[0]
        nb = seq // tm
        in_specs += [pl.BlockSpec((tm, dh), lambda i: (i % nb, 0))] * 2
        args += list(rope_tables)
    out_shape = [jax.ShapeDtypeStruct((t, branch), bf16), jax.ShapeDtypeStruct((t, branch), bf16),
                 jax.ShapeDtypeStruct((t, kv_w), bf16), jax.ShapeDtypeStruct((t, kv_w), bf16)]
    out_specs = [pl.BlockSpec((tm, branch), lambda i: (i, 0)), pl.BlockSpec((tm, branch), lambda i: (i, 0)),
                 pl.BlockSpec((tm, kv_w), lambda i: (i, 0)), pl.BlockSpec((tm, kv_w), lambda i: (i, 0))]
    if cache_out:
        out_shape += [jax.ShapeDtypeStruct((t, kv_w), f32)] * 2
        out_specs += [pl.BlockSpec((tm, kv_w), lambda i: (i, 0))] * 2
    kern = functools.partial(_in_proj_kernel, branch=branch, kv_w=kv_w, dh=dh, chunk=chunk, rope=rope,
                             cache_out=cache_out, q_scale=q_scale)
    return pl.pallas_call(
        kern, grid=(t // tm,), in_specs=in_specs, out_specs=out_specs, out_shape=out_shape,
        compiler_params=_params(1), name="in_proj_latent" if rope else "in_proj_context",
    )(*args)


def _stack_groups(qb, groups, dh):
    return jnp.concatenate([qb[:, g * dh:(g + 1) * dh] for g in range(groups)], axis=0)


def _sink_column(sink_ref, head0, groups, rows):
    rowg = lax.broadcasted_iota(jnp.int32, (groups * rows, 1), 0) // rows
    col = jnp.zeros((groups * rows, 1), f32)
    for g in range(groups):
        col = jnp.where(rowg == g, sink_ref[head0 + g], col)
    return col


def _store_gated(o_ref, row0, rows, o, szb, groups, dh):
    for g in range(groups):
        u = o[g * rows:(g + 1) * rows] * szb[:, g * dh:(g + 1) * dh].astype(f32)
        o_ref[pl.ds(row0, rows), g * dh:(g + 1) * dh] = u.astype(bf16)


def _attn_context_kernel(*refs, groups, dh, use_sink):
    if use_sink:
        sink_ref, q_ref, sz_ref, k_ref, v_ref, o_ref = refs
    else:
        q_ref, sz_ref, k_ref, v_ref, o_ref = refs
    rows = q_ref.shape[0]
    qs = _stack_groups(q_ref[...], groups, dh)
    s = lax.dot_general(qs, k_ref[...], _NT, preferred_element_type=f32)
    m = jnp.max(s, axis=-1, keepdims=True)
    if use_sink:
        sink = _sink_column(sink_ref, pl.program_id(1) * groups, groups, rows)
        m = jnp.maximum(m, sink)
    p = jnp.exp(s - m)
    l = jnp.sum(p, axis=-1, keepdims=True)
    if use_sink:
        l = l + jnp.exp(sink - m)
    o = jnp.dot(p.astype(bf16), v_ref[...], preferred_element_type=f32) / l
    _store_gated(o_ref, 0, rows, o, sz_ref[...], groups, dh)


def _attn_context(q, sz, k, v, sink, *, batch, seq, groups, dh):
    t, branch = q.shape
    n_kv = k.shape[1] // dh
    gw = groups * dh
    use_sink = sink is not None
    in_specs = [
        pl.BlockSpec((seq, gw), lambda b, h: (b, h)),
        pl.BlockSpec((seq, gw), lambda b, h: (b, h)),
        pl.BlockSpec((seq, dh), lambda b, h: (b, h)),
        pl.BlockSpec((seq, dh), lambda b, h: (b, h)),
    ]
    args = [q, sz, k, v]
    if use_sink:
        in_specs = [pl.BlockSpec(memory_space=pltpu.SMEM)] + in_specs
        args = [sink] + args
    return pl.pallas_call(
        functools.partial(_attn_context_kernel, groups=groups, dh=dh, use_sink=use_sink),
        grid=(batch, n_kv), in_specs=in_specs,
        out_specs=pl.BlockSpec((seq, gw), lambda b, h: (b, h)),
        out_shape=jax.ShapeDtypeStruct((t, branch), bf16),
        compiler_params=_params(2), name="attn_context",
    )(*args)


def _attn_window_kernel(sink_ref, q_ref, sz_ref, k_ref, v_ref, kc_ref, vc_ref, o_ref, *, groups, dh, tq, window):
    seq = q_ref.shape[0]
    span = tq + 2 * window
    kc = kc_ref[...].astype(bf16)
    vc = vc_ref[...].astype(bf16)
    sink = _sink_column(sink_ref, pl.program_id(1) * groups, groups, tq)
    row_tok = lax.broadcasted_iota(jnp.int32, (groups * tq, span), 0) % tq
    rel = row_tok - lax.broadcasted_iota(jnp.int32, (groups * tq, span), 1)

    def body(i, carry):
        q0 = pl.multiple_of(i * tq, tq)
        k0 = pl.multiple_of(jnp.clip(q0 - window, 0, seq - span), tq)
        qs = _stack_groups(q_ref[pl.ds(q0, tq), :], groups, dh)
        s1 = lax.dot_general(qs, k_ref[pl.ds(k0, span), :], _NT, preferred_element_type=f32)
        dist = rel + (q0 - k0)
        s1 = jnp.where((dist >= -window) & (dist <= window), s1, NEG_INF)
        s2 = lax.dot_general(qs, kc, _NT, preferred_element_type=f32)
        m = jnp.maximum(jnp.maximum(jnp.max(s1, axis=-1, keepdims=True),
                                    jnp.max(s2, axis=-1, keepdims=True)), sink)
        p1 = jnp.exp(s1 - m)
        p2 = jnp.exp(s2 - m)
        l = jnp.sum(p1, axis=-1, keepdims=True) + jnp.sum(p2, axis=-1, keepdims=True) + jnp.exp(sink - m)
        o = (jnp.dot(p1.astype(bf16), v_ref[pl.ds(k0, span), :], preferred_element_type=f32)
             + jnp.dot(p2.astype(bf16), vc, preferred_element_type=f32)) / l
        _store_gated(o_ref, q0, tq, o, sz_ref[pl.ds(q0, tq), :], groups, dh)
        return carry

    lax.fori_loop(0, seq // tq, body, 0)


def _attn_window(q, sz, k, v, cache_k4, cache_v4, sink, layer, *, batch, seq, groups, dh):
    t, branch = q.shape
    n_kv = k.shape[1] // dh
    gw = groups * dh
    past = cache_k4.shape[2]
    assert seq >= Q_BLOCK + 2 * WINDOW and seq % Q_BLOCK == 0 and WINDOW % Q_BLOCK == 0
    ctx_spec = pl.BlockSpec((None, None, past, dh), lambda b, h: (b, layer, 0, h))
    return pl.pallas_call(
        functools.partial(_attn_window_kernel, groups=groups, dh=dh, tq=Q_BLOCK, window=WINDOW),
        grid=(batch, n_kv),
        in_specs=[
            pl.BlockSpec(memory_space=pltpu.SMEM),
            pl.BlockSpec((seq, gw), lambda b, h: (b, h)),
            pl.BlockSpec((seq, gw), lambda b, h: (b, h)),
            pl.BlockSpec((seq, dh), lambda b, h: (b, h)),
            pl.BlockSpec((seq, dh), lambda b, h: (b, h)),
            ctx_spec, ctx_spec,
        ],
        out_specs=pl.BlockSpec((seq, gw), lambda b, h: (b, h)),
        out_shape=jax.ShapeDtypeStruct((t, branch), bf16),
        compiler_params=_params(2), name="attn_window",
    )(sink, q, sz, k, v, cache_k4, cache_v4)


def _attn_dense_kernel(q_ref, sz_ref, k_ref, v_ref, kc_ref, vc_ref, o_ref, kall_ref, vall_ref, s_ref, m_ref, *,
                       groups, dh, tq):
    seq = k_ref.shape[0]
    kall_ref[0:seq, :] = k_ref[...]
    kall_ref[seq:, :] = kc_ref[...].astype(bf16)
    vall_ref[0:seq, 0:dh] = v_ref[...]
    vall_ref[seq:, 0:dh] = vc_ref[...].astype(bf16)
    vall_ref[:, dh:] = jnp.ones((vall_ref.shape[0], dh), bf16)

    def scores(q0):
        qs = _stack_groups(q_ref[pl.ds(q0, tq), :], groups, dh)
        s = lax.dot_general(qs, kall_ref[...], _NT, preferred_element_type=f32)
        s_ref[...] = s
        m_ref[...] = jnp.max(s, axis=-1, keepdims=True)

    scores(0)

    def body(i, carry):
        q0 = pl.multiple_of(i * tq, tq)
        p = jnp.exp2(s_ref[...] - m_ref[...])
        oa = jnp.dot(p.astype(bf16), vall_ref[...], preferred_element_type=f32)
        o = oa[:, :dh] / oa[:, dh:]
        _store_gated(o_ref, q0, tq, o, sz_ref[pl.ds(q0, tq), :], groups, dh)
        scores(pl.multiple_of(jnp.minimum(q0 + tq, seq - tq), tq))
        return carry

    lax.fori_loop(0, seq // tq, body, 0)


def _attn_dense(q, sz, k, v, cache_k4, cache_v4, layer, *, batch, seq, groups, dh, tq):
    t, branch = q.shape
    n_kv = k.shape[1] // dh
    gw = groups * dh
    past = cache_k4.shape[2]
    ctx_spec = pl.BlockSpec((None, None, past, dh), lambda b, h: (b, layer, 0, h))
    return pl.pallas_call(
        functools.partial(_attn_dense_kernel, groups=groups, dh=dh, tq=tq),
        grid=(batch, n_kv),
        in_specs=[
            pl.BlockSpec((seq, gw), lambda b, h: (b, h)),
            pl.BlockSpec((seq, gw), lambda b, h: (b, h)),
            pl.BlockSpec((seq, dh), lambda b, h: (b, h)),
            pl.BlockSpec((seq, dh), lambda b, h: (b, h)),
            ctx_spec, ctx_spec,
        ],
        out_specs=pl.BlockSpec((seq, gw), lambda b, h: (b, h)),
        out_shape=jax.ShapeDtypeStruct((t, branch), bf16),
        scratch_shapes=[pltpu.VMEM((seq + past, dh), bf16), pltpu.VMEM((seq + past, 2 * dh), bf16),
                        pltpu.VMEM((groups * tq, seq + past), f32), pltpu.VMEM((groups * tq, 1), f32)],
        compiler_params=_params(2), name="attn_dense",
    )(q, sz, k, v, cache_k4, cache_v4)


def _out_proj_kernel(u_ref, x_ref, gate_ref, w_ref, o_ref):
    y = jnp.dot(u_ref[...], w_ref[...], preferred_element_type=f32)
    o_ref[...] = x_ref[...] + gate_ref[...] * y


def _out_proj(u, x, mod4, w_out, layer, *, tm, rows_per_mod, mod_row0):
    t, d = x.shape
    branch = u.shape[1]

    def mod_row(i):
        return mod_row0 + (i * tm) // rows_per_mod if rows_per_mod else mod_row0

    return pl.pallas_call(
        _out_proj_kernel, grid=(t // tm,),
        in_specs=[
            pl.BlockSpec((tm, branch), lambda i: (i, 0)),
            pl.BlockSpec((tm, d), lambda i: (i, 0)),
            pl.BlockSpec((None, None, 1, d), lambda i: (layer, mod_row(i), 0, 2)),
            pl.BlockSpec((None, branch, d), lambda i: (layer, 0, 0), pipeline_mode=pl.Buffered(1)),
        ],
        out_specs=pl.BlockSpec((tm, d), lambda i: (i, 0)),
        out_shape=jax.ShapeDtypeStruct((t, d), f32),
        compiler_params=_params(1), name="out_proj",
    )(u, x, mod4, w_out)


def _rope_tables(n_tokens, dh):
    rows = n_tokens // GRID_W
    row = jnp.repeat(jnp.arange(rows), GRID_W).astype(f32)
    col = jnp.tile(jnp.arange(GRID_W), rows).astype(f32)
    n_freq = dh // 4
    inv = ROPE_THETA ** (-jnp.arange(n_freq, dtype=f32) / n_freq)
    ar, ac = row[:, None] * inv, col[:, None] * inv
    cos = jnp.concatenate([jnp.cos(ar), jnp.cos(ar), jnp.cos(ac), jnp.cos(ac)], axis=1)
    sin = jnp.concatenate([-jnp.sin(ar), jnp.sin(ar), -jnp.sin(ac), jnp.sin(ac)], axis=1)
    return cos, sin


def kernel(x_prompt, x_sample, c, cache_k, cache_v, c_ctx, norm_g, w_mod, b_mod, w_in,
           q_norm_g, k_norm_g, sinks, w_out):
    batch, seq, d = x_prompt.shape
    dec_batch, dec_seq, _ = x_sample.shape
    depth = w_in.shape[0]
    past, n_kv, dh = cache_k.shape[2:]
    n_heads = sinks.shape[1]
    groups = n_heads // n_kv
    branch = n_heads * dh
    kv_w = n_kv * dh
    assert dec_batch + 1 <= MOD_ROWS

    cc = jnp.concatenate([c_ctx[None], c, jnp.zeros((MOD_ROWS - 1 - dec_batch, d), f32)], axis=0)
    mod4 = _modulation(cc, w_mod, b_mod).reshape(depth, MOD_ROWS, 1, 3 * d)

    w_in_b = w_in.astype(bf16)
    w_out_b = w_out.astype(bf16)
    norm_g3 = norm_g.reshape(depth, 1, d)
    qg3 = q_norm_g.reshape(depth, 1, dh)
    kg3 = k_norm_g.reshape(depth, 1, dh)
    cache_k4 = cache_k.reshape(dec_batch, depth, past, kv_w)
    cache_v4 = cache_v.reshape(dec_batch, depth, past, kv_w)
    rope = _rope_tables(dec_seq, dh)

    xp = x_prompt.reshape(batch * seq, d)
    xs = x_sample.reshape(dec_batch * dec_seq, d)
    tm = 256
    dims = dict(branch=branch, kv_w=kv_w, dh=dh)
    sm_scale = dh ** -0.5
    new_k, new_v = [], []
    for layer in range(depth):
        use_window = layer % 2 == 0
        sink = sinks[layer // 2] if use_window else None

        q, sz, k, v, kf, vf = _in_proj(xp, mod4, norm_g3, w_in_b, qg3, kg3, layer, tm=tm, rows_per_mod=0,
                                       mod_row0=0, cache_out=True, q_scale=sm_scale, **dims)
        u = _attn_context(q, sz, k, v, sink, batch=batch, seq=seq, groups=groups, dh=dh)
        xp = _out_proj(u, xp, mod4, w_out_b, layer, tm=tm, rows_per_mod=0, mod_row0=0)
        new_k.append(kf.reshape(batch, seq, n_kv, dh))
        new_v.append(vf.reshape(batch, seq, n_kv, dh))

        q, sz, k, v = _in_proj(xs, mod4, norm_g3, w_in_b, qg3, kg3, layer, tm=tm, rows_per_mod=dec_seq,
                               mod_row0=1, rope_tables=rope,
                               q_scale=sm_scale if use_window else sm_scale * LOG2_E, **dims)
        if use_window:
            u = _attn_window(q, sz, k, v, cache_k4, cache_v4, sink, layer, batch=dec_batch, seq=dec_seq,
                             groups=groups, dh=dh)
        else:
            u = _attn_dense(q, sz, k, v, cache_k4, cache_v4, layer, batch=dec_batch, seq=dec_seq,
                            groups=groups, dh=dh, tq=64)
        xs = _out_proj(u, xs, mod4, w_out_b, layer, tm=tm, rows_per_mod=dec_seq, mod_row0=1)

    return (xp.reshape(batch, seq, d), xs.reshape(dec_batch, dec_seq, d),
            jnp.stack(new_k, axis=1), jnp.stack(new_v, axis=1))
```

```python
import functools

import jax
import jax.numpy as jnp
from jax import lax
from jax.experimental import pallas as pl
from jax.experimental.pallas import tpu as pltpu

GRID_W = 64
WINDOW = 128
Q_BLOCK = 128
ROPE_THETA = 10000.0
EPS = 1e-6
NEG_INF = -1e30
LOG2_E = 1.4426950408889634

VMEM_LIMIT_BYTES = 56 * 1024 * 1024
MOD_ROWS = 16
PROJ_ROWS = 256
DENSE_Q_ROWS = 64

bf16 = jnp.bfloat16
f32 = jnp.float32

_NT = (((1,), (1,)), ((), ()))


def _params(n_axes):
    return pltpu.CompilerParams(
        dimension_semantics=("arbitrary",) * n_axes, vmem_limit_bytes=VMEM_LIMIT_BYTES)


def _mod_kernel(cc_ref, w_ref, b_ref, o_ref):
    cc = cc_ref[...]
    a = (cc * jax.nn.sigmoid(cc)).astype(bf16)
    o_ref[...] = jnp.dot(a, w_ref[...].astype(bf16), preferred_element_type=f32) + b_ref[...]


def _modulation(cc, w_mod, b_mod):
    depth, d, n = w_mod.shape
    tn = 1024 if n % 1024 == 0 else n
    return pl.pallas_call(
        _mod_kernel,
        grid=(depth, n // tn),
        in_specs=[
            pl.BlockSpec((MOD_ROWS, d), lambda l, j: (0, 0)),
            pl.BlockSpec((None, d, tn), lambda l, j: (l, 0, j)),
            pl.BlockSpec((None, 1, tn), lambda l, j: (l, 0, j)),
        ],
        out_specs=pl.BlockSpec((None, MOD_ROWS, tn), lambda l, j: (l, 0, j)),
        out_shape=jax.ShapeDtypeStruct((depth, MOD_ROWS, n), f32),
        compiler_params=_params(2),
        name="modulation",
    )(cc, w_mod, b_mod.reshape(depth, 1, n))


def _head_rms(t, g):
    ms = jnp.mean(t * t, axis=-1, keepdims=True)
    return t * lax.rsqrt(ms + EPS) * g


def _in_proj_kernel(*refs, branch, kv_w, dh, chunk, rope, cache_out):
    x_ref, shift_ref, scale_ref, ng_ref, w_ref, qg_ref, kg_ref = refs[:7]
    refs = refs[7:]
    if rope:
        cos_ref, sin_ref = refs[:2]
        refs = refs[2:]
    q_ref, sz_ref, k_ref, v_ref = refs[:4]
    if cache_out:
        kf_ref, vf_ref = refs[4:6]

    x = x_ref[...]
    ms = jnp.mean(x * x, axis=-1, keepdims=True)
    h = x * lax.rsqrt(ms + EPS) * ng_ref[...]
    h = h * (1.0 + scale_ref[...]) + shift_ref[...]
    hb = h.astype(bf16)

    if rope:
        cos = cos_ref[...]
        sin = sin_ref[...]
        lane = lax.broadcasted_iota(jnp.int32, cos.shape, 1)
        first_half = (lane // (dh // 4)) % 2 == 0

        def rotary(t):
            partner = jnp.where(first_half, pltpu.roll(t, dh - dh // 4, 1), pltpu.roll(t, dh // 4, 1))
            return t * cos + partner * sin
    else:
        def rotary(t):
            return t

    qg = qg_ref[...]
    kg = kg_ref[...]
    q_scale = LOG2_E * dh ** -0.5
    heads_per_chunk = chunk // dh
    for c in range((2 * branch + 2 * kv_w) // chunk):
        col = c * chunk
        r = jnp.dot(hb, w_ref[:, col:col + chunk], preferred_element_type=f32)
        if col < branch:
            for j in range(heads_per_chunk):
                t = rotary(_head_rms(r[:, j * dh:(j + 1) * dh], qg)) * q_scale
                q_ref[:, col + j * dh:col + (j + 1) * dh] = t.astype(bf16)
        elif col < 2 * branch:
            sz_ref[:, col - branch:col - branch + chunk] = (r * jax.nn.sigmoid(r)).astype(bf16)
        elif col < 2 * branch + kv_w:
            base = col - 2 * branch
            for j in range(heads_per_chunk):
                t = _head_rms(r[:, j * dh:(j + 1) * dh], kg)
                if cache_out:
                    kf_ref[:, base + j * dh:base + (j + 1) * dh] = t
                k_ref[:, base + j * dh:base + (j + 1) * dh] = rotary(t).astype(bf16)
        else:
            base = col - 2 * branch - kv_w
            if cache_out:
                vf_ref[:, base:base + chunk] = r
            v_ref[:, base:base + chunk] = r.astype(bf16)


def _in_proj(x, mod4, norm_g3, w_in, qg3, kg3, layer, *, rows_per_mod, mod_row0, branch, kv_w, dh,
             rope_tables=None, cache_out=False):
    t, d = x.shape
    tm = PROJ_ROWS
    proj_w = w_in.shape[-1]
    chunk = min(512, kv_w)
    rope = rope_tables is not None

    def mod_row(i):
        return mod_row0 + (i * tm) // rows_per_mod if rows_per_mod else mod_row0

    in_specs = [
        pl.BlockSpec((tm, d), lambda i: (i, 0)),
        pl.BlockSpec((None, None, 1, d), lambda i: (layer, mod_row(i), 0, 0)),
        pl.BlockSpec((None, None, 1, d), lambda i: (layer, mod_row(i), 0, 1)),
        pl.BlockSpec((None, 1, d), lambda i: (layer, 0, 0)),
        pl.BlockSpec((None, d, proj_w), lambda i: (layer, 0, 0), pipeline_mode=pl.Buffered(1)),
        pl.BlockSpec((None, 1, dh), lambda i: (layer, 0, 0)),
        pl.BlockSpec((None, 1, dh), lambda i: (layer, 0, 0)),
    ]
    args = [x, mod4, mod4, norm_g3, w_in, qg3, kg3]
    if rope:
        seq = rope_tables[0].shape[0]
        nb = seq // tm
        in_specs += [pl.BlockSpec((tm, dh), lambda i: (i % nb, 0))] * 2
        args += list(rope_tables)
    out_shape = [jax.ShapeDtypeStruct((t, branch), bf16), jax.ShapeDtypeStruct((t, branch), bf16),
                 jax.ShapeDtypeStruct((t, kv_w), bf16), jax.ShapeDtypeStruct((t, kv_w), bf16)]
    out_specs = [pl.BlockSpec((tm, branch), lambda i: (i, 0)), pl.BlockSpec((tm, branch), lambda i: (i, 0)),
                 pl.BlockSpec((tm, kv_w), lambda i: (i, 0)), pl.BlockSpec((tm, kv_w), lambda i: (i, 0))]
    if cache_out:
        out_shape += [jax.ShapeDtypeStruct((t, kv_w), f32)] * 2
        out_specs += [pl.BlockSpec((tm, kv_w), lambda i: (i, 0))] * 2
    kern = functools.partial(_in_proj_kernel, branch=branch, kv_w=kv_w, dh=dh, chunk=chunk, rope=rope,
                             cache_out=cache_out)
    return pl.pallas_call(
        kern, grid=(t // tm,), in_specs=in_specs, out_specs=out_specs, out_shape=out_shape,
        compiler_params=_params(1), name="in_proj_latent" if rope else "in_proj_context",
    )(*args)


def _stack_groups(qb, groups, dh):
    return jnp.concatenate([qb[:, g * dh:(g + 1) * dh] for g in range(groups)], axis=0)


def _sink_column(sink_ref, head0, groups, rows):
    rowg = lax.broadcasted_iota(jnp.int32, (groups * rows, 1), 0) // rows
    col = jnp.zeros((groups * rows, 1), f32)
    for g in range(groups):
        col = jnp.where(rowg == g, sink_ref[head0 + g], col)
    return col * LOG2_E


def _store_gated(o_ref, row0, rows, col0, o, szb, groups, dh):
    for g in range(groups):
        u = o[g * rows:(g + 1) * rows] * szb[:, g * dh:(g + 1) * dh].astype(f32)
        o_ref[pl.ds(row0, rows), col0 + g * dh:col0 + (g + 1) * dh] = u.astype(bf16)


def _pipeline_blocks(seq, tq, scores, row_max, attend):
    n = seq // tq
    assert n % 2 == 0 and n >= 2
    last = seq - tq

    def start(q):
        return pl.multiple_of(jnp.minimum(q, last), tq)

    scores(0, 0)
    row_max(0)
    scores(tq, 1)

    def body(j, carry):
        q0 = pl.multiple_of(2 * j * tq, tq)
        attend(q0, 0)
        row_max(1)
        scores(start(q0 + 2 * tq), 0)
        attend(q0 + tq, 1)
        row_max(0)
        scores(start(q0 + 3 * tq), 1)
        return carry

    lax.fori_loop(0, n // 2, body, 0)


def _attn_context_kernel(*refs, groups, dh, use_sink):
    if use_sink:
        sink_ref, q_ref, sz_ref, k_ref, v_ref, o_ref = refs
    else:
        q_ref, sz_ref, k_ref, v_ref, o_ref = refs
    rows = q_ref.shape[0]
    gw = groups * dh
    ones = jnp.ones((rows, dh), bf16)
    for h in range(k_ref.shape[1] // dh):
        qs = _stack_groups(q_ref[:, h * gw:(h + 1) * gw], groups, dh)
        s = lax.dot_general(qs, k_ref[:, h * dh:(h + 1) * dh], _NT, preferred_element_type=f32)
        m = jnp.max(s, axis=-1, keepdims=True)
        if use_sink:
            sink = _sink_column(sink_ref, h * groups, groups, rows)
            m = jnp.maximum(m, sink)
        p = jnp.exp2(s - m)
        v_aug = jnp.concatenate([v_ref[:, h * dh:(h + 1) * dh], ones], axis=1)
        oa = jnp.dot(p.astype(bf16), v_aug, preferred_element_type=f32)
        den = oa[:, dh:]
        if use_sink:
            den = den + jnp.exp2(sink - m)
        _store_gated(o_ref, 0, rows, h * gw, oa[:, :dh] / den, sz_ref[:, h * gw:(h + 1) * gw], groups, dh)


def _attn_context(q, sz, k, v, sink, *, batch, seq, groups, dh):
    t, branch = q.shape
    kv_w = k.shape[1]
    use_sink = sink is not None
    in_specs = [
        pl.BlockSpec((seq, branch), lambda b: (b, 0)),
        pl.BlockSpec((seq, branch), lambda b: (b, 0)),
        pl.BlockSpec((seq, kv_w), lambda b: (b, 0)),
        pl.BlockSpec((seq, kv_w), lambda b: (b, 0)),
    ]
    args = [q, sz, k, v]
    if use_sink:
        in_specs = [pl.BlockSpec(memory_space=pltpu.SMEM)] + in_specs
        args = [sink] + args
    return pl.pallas_call(
        functools.partial(_attn_context_kernel, groups=groups, dh=dh, use_sink=use_sink),
        grid=(batch,), in_specs=in_specs,
        out_specs=pl.BlockSpec((seq, branch), lambda b: (b, 0)),
        out_shape=jax.ShapeDtypeStruct((t, branch), bf16),
        compiler_params=_params(1), name="attn_context",
    )(*args)


def _attn_window_kernel(sink_ref, q_ref, sz_ref, k_ref, v_ref, kc_ref, vc_ref, o_ref,
                        kcb_ref, vca_ref, va_ref, s_ref, m_ref, *, groups, dh, tq, window):
    seq = q_ref.shape[0]
    past = kc_ref.shape[0]
    span = tq + 2 * window
    rows = groups * tq
    kcb_ref[...] = kc_ref[...].astype(bf16)
    vca_ref[:, 0:dh] = vc_ref[...].astype(bf16)
    vca_ref[:, dh:] = jnp.ones((past, dh), bf16)
    va_ref[:, 0:dh] = v_ref[...]
    va_ref[:, dh:] = jnp.ones((seq, dh), bf16)
    sink = _sink_column(sink_ref, pl.program_id(1) * groups, groups, tq)
    row_tok = lax.broadcasted_iota(jnp.int32, (rows, span), 0) % tq
    rel = row_tok - lax.broadcasted_iota(jnp.int32, (rows, span), 1)

    def key_start(q0):
        return pl.multiple_of(jnp.clip(q0 - window, 0, seq - span), tq)

    def scores(q0, slot):
        k0 = key_start(q0)
        qs = _stack_groups(q_ref[pl.ds(q0, tq), :], groups, dh)
        s_ctx = lax.dot_general(qs, kcb_ref[...], _NT, preferred_element_type=f32)
        s_win = lax.dot_general(qs, k_ref[pl.ds(k0, span), :], _NT, preferred_element_type=f32)
        dist = rel + (q0 - k0)
        s_win = jnp.where((dist >= -window) & (dist <= window), s_win, NEG_INF)
        s_ref[slot] = jnp.concatenate([s_ctx, s_win], axis=1)

    def row_max(slot):
        m_ref[slot] = jnp.maximum(jnp.max(s_ref[slot], axis=-1, keepdims=True), sink)

    def attend(q0, slot):
        k0 = key_start(q0)
        m = m_ref[slot]
        p = jnp.exp2(s_ref[slot] - m).astype(bf16)
        oa = (jnp.dot(p[:, :past], vca_ref[...], preferred_element_type=f32)
              + jnp.dot(p[:, past:], va_ref[pl.ds(k0, span), :], preferred_element_type=f32))
        o = oa[:, :dh] / (oa[:, dh:] + jnp.exp2(sink - m))
        _store_gated(o_ref, q0, tq, 0, o, sz_ref[pl.ds(q0, tq), :], groups, dh)

    _pipeline_blocks(seq, tq, scores, row_max, attend)


def _attn_window(q, sz, k, v, cache_k4, cache_v4, sink, layer, *, batch, seq, groups, dh):
    t, branch = q.shape
    n_kv = k.shape[1] // dh
    gw = groups * dh
    past = cache_k4.shape[2]
    span = Q_BLOCK + 2 * WINDOW
    assert seq >= span and seq % Q_BLOCK == 0 and WINDOW % Q_BLOCK == 0
    ctx_spec = pl.BlockSpec((None, None, past, dh), lambda b, h: (b, layer, 0, h))
    return pl.pallas_call(
        functools.partial(_attn_window_kernel, groups=groups, dh=dh, tq=Q_BLOCK, window=WINDOW),
        grid=(batch, n_kv),
        in_specs=[
            pl.BlockSpec(memory_space=pltpu.SMEM),
            pl.BlockSpec((seq, gw), lambda b, h: (b, h)),
            pl.BlockSpec((seq, gw), lambda b, h: (b, h)),
            pl.BlockSpec((seq, dh), lambda b, h: (b, h)),
            pl.BlockSpec((seq, dh), lambda b, h: (b, h)),
            ctx_spec, ctx_spec,
        ],
        out_specs=pl.BlockSpec((seq, gw), lambda b, h: (b, h)),
        out_shape=jax.ShapeDtypeStruct((t, branch), bf16),
        scratch_shapes=[pltpu.VMEM((past, dh), bf16), pltpu.VMEM((past, 2 * dh), bf16),
                        pltpu.VMEM((seq, 2 * dh), bf16),
                        pltpu.VMEM((2, groups * Q_BLOCK, past + span), f32),
                        pltpu.VMEM((2, groups * Q_BLOCK, 1), f32)],
        compiler_params=_params(2), name="attn_window",
    )(sink, q, sz, k, v, cache_k4, cache_v4)


def _attn_dense_kernel(q_ref, sz_ref, k_ref, v_ref, kc_ref, vc_ref, o_ref, kall_ref, vall_ref, s_ref, m_ref, *,
                       groups, dh, tq):
    seq = k_ref.shape[0]
    kall_ref[0:seq, :] = k_ref[...]
    kall_ref[seq:, :] = kc_ref[...].astype(bf16)
    vall_ref[0:seq, 0:dh] = v_ref[...]
    vall_ref[seq:, 0:dh] = vc_ref[...].astype(bf16)
    vall_ref[:, dh:] = jnp.ones((vall_ref.shape[0], dh), bf16)

    def scores(q0, slot):
        qs = _stack_groups(q_ref[pl.ds(q0, tq), :], groups, dh)
        s_ref[slot] = lax.dot_general(qs, kall_ref[...], _NT, preferred_element_type=f32)

    def row_max(slot):
        m_ref[slot] = jnp.max(s_ref[slot], axis=-1, keepdims=True)

    def attend(q0, slot):
        p = jnp.exp2(s_ref[slot] - m_ref[slot])
        oa = jnp.dot(p.astype(bf16), vall_ref[...], preferred_element_type=f32)
        o = oa[:, :dh] / oa[:, dh:]
        _store_gated(o_ref, q0, tq, 0, o, sz_ref[pl.ds(q0, tq), :], groups, dh)

    _pipeline_blocks(seq, tq, scores, row_max, attend)


def _attn_dense(q, sz, k, v, cache_k4, cache_v4, layer, *, batch, seq, groups, dh):
    t, branch = q.shape
    n_kv = k.shape[1] // dh
    gw = groups * dh
    past = cache_k4.shape[2]
    tq = DENSE_Q_ROWS
    ctx_spec = pl.BlockSpec((None, None, past, dh), lambda b, h: (b, layer, 0, h))
    return pl.pallas_call(
        functools.partial(_attn_dense_kernel, groups=groups, dh=dh, tq=tq),
        grid=(batch, n_kv),
        in_specs=[
            pl.BlockSpec((seq, gw), lambda b, h: (b, h)),
            pl.BlockSpec((seq, gw), lambda b, h: (b, h)),
            pl.BlockSpec((seq, dh), lambda b, h: (b, h)),
            pl.BlockSpec((seq, dh), lambda b, h: (b, h)),
            ctx_spec, ctx_spec,
        ],
        out_specs=pl.BlockSpec((seq, gw), lambda b, h: (b, h)),
        out_shape=jax.ShapeDtypeStruct((t, branch), bf16),
        scratch_shapes=[pltpu.VMEM((seq + past, dh), bf16), pltpu.VMEM((seq + past, 2 * dh), bf16),
                        pltpu.VMEM((2, groups * tq, seq + past), f32), pltpu.VMEM((2, groups * tq, 1), f32)],
        compiler_params=_params(2), name="attn_dense",
    )(q, sz, k, v, cache_k4, cache_v4)


def _out_proj_kernel(u_ref, x_ref, gate_ref, w_ref, o_ref):
    y = jnp.dot(u_ref[...], w_ref[...], preferred_element_type=f32)
    o_ref[...] = x_ref[...] + gate_ref[...] * y


def _out_proj(u, x, mod4, w_out, layer, *, rows_per_mod, mod_row0):
    t, d = x.shape
    tm = PROJ_ROWS
    branch = u.shape[1]

    def mod_row(i):
        return mod_row0 + (i * tm) // rows_per_mod if rows_per_mod else mod_row0

    return pl.pallas_call(
        _out_proj_kernel, grid=(t // tm,),
        in_specs=[
            pl.BlockSpec((tm, branch), lambda i: (i, 0)),
            pl.BlockSpec((tm, d), lambda i: (i, 0)),
            pl.BlockSpec((None, None, 1, d), lambda i: (layer, mod_row(i), 0, 2)),
            pl.BlockSpec((None, branch, d), lambda i: (layer, 0, 0), pipeline_mode=pl.Buffered(1)),
        ],
        out_specs=pl.BlockSpec((tm, d), lambda i: (i, 0)),
        out_shape=jax.ShapeDtypeStruct((t, d), f32),
        compiler_params=_params(1), name="out_proj",
    )(u, x, mod4, w_out)


def _rope_tables(n_tokens, dh):
    rows = n_tokens // GRID_W
    row = jnp.repeat(jnp.arange(rows), GRID_W).astype(f32)
    col = jnp.tile(jnp.arange(GRID_W), rows).astype(f32)
    n_freq = dh // 4
    inv = ROPE_THETA ** (-jnp.arange(n_freq, dtype=f32) / n_freq)
    ar, ac = row[:, None] * inv, col[:, None] * inv
    cos = jnp.concatenate([jnp.cos(ar), jnp.cos(ar), jnp.cos(ac), jnp.cos(ac)], axis=1)
    sin = jnp.concatenate([-jnp.sin(ar), jnp.sin(ar), -jnp.sin(ac), jnp.sin(ac)], axis=1)
    return cos, sin


def kernel(x_prompt, x_sample, c, cache_k, cache_v, c_ctx, norm_g, w_mod, b_mod, w_in,
           q_norm_g, k_norm_g, sinks, w_out):
    batch, seq, d = x_prompt.shape
    dec_batch, dec_seq, _ = x_sample.shape
    depth = w_in.shape[0]
    past, n_kv, dh = cache_k.shape[2:]
    n_heads = sinks.shape[1]
    groups = n_heads // n_kv
    branch = n_heads * dh
    kv_w = n_kv * dh
    assert dec_batch + 1 <= MOD_ROWS

    cc = jnp.concatenate([c_ctx[None], c, jnp.zeros((MOD_ROWS - 1 - dec_batch, d), f32)], axis=0)
    mod4 = _modulation(cc, w_mod, b_mod).reshape(depth, MOD_ROWS, 1, 3 * d)

    w_in_b = w_in.astype(bf16)
    w_out_b = w_out.astype(bf16)
    norm_g3 = norm_g.reshape(depth, 1, d)
    qg3 = q_norm_g.reshape(depth, 1, dh)
    kg3 = k_norm_g.reshape(depth, 1, dh)
    cache_k4 = cache_k.reshape(dec_batch, depth, past, kv_w)
    cache_v4 = cache_v.reshape(dec_batch, depth, past, kv_w)
    rope = _rope_tables(dec_seq, dh)

    xp = x_prompt.reshape(batch * seq, d)
    xs = x_sample.reshape(dec_batch * dec_seq, d)
    dims = dict(branch=branch, kv_w=kv_w, dh=dh)
    new_k, new_v = [], []
    for layer in range(depth):
        use_window = layer % 2 == 0
        sink = sinks[layer // 2] if use_window else None

        q, sz, k, v, kf, vf = _in_proj(xp, mod4, norm_g3, w_in_b, qg3, kg3, layer, rows_per_mod=0,
                                       mod_row0=0, cache_out=True, **dims)
        u = _attn_context(q, sz, k, v, sink, batch=batch, seq=seq, groups=groups, dh=dh)
        xp = _out_proj(u, xp, mod4, w_out_b, layer, rows_per_mod=0, mod_row0=0)
        new_k.append(kf.reshape(batch, seq, n_kv, dh))
        new_v.append(vf.reshape(batch, seq, n_kv, dh))

        q, sz, k, v = _in_proj(xs, mod4, norm_g3, w_in_b, qg3, kg3, layer, rows_per_mod=dec_seq,
                               mod_row0=1, rope_tables=rope, **dims)
        if use_window:
            u = _attn_window(q, sz, k, v, cache_k4, cache_v4, sink, layer, batch=dec_batch, seq=dec_seq,
                             groups=groups, dh=dh)
        else:
            u = _attn_dense(q, sz, k, v, cache_k4, cache_v4, layer, batch=dec_batch, seq=dec_seq,
                            groups=groups, dh=dh)
        xs = _out_proj(u, xs, mod4, w_out_b, layer, rows_per_mod=dec_seq, mod_row0=1)

    return (xp.reshape(batch, seq, d), xs.reshape(dec_batch, dec_seq, d),
            jnp.stack(new_k, axis=1), jnp.stack(new_v, axis=1))
```

```python
import functools

import jax
import jax.numpy as jnp
from jax import lax
from jax.experimental import pallas as pl
from jax.experimental.pallas import tpu as pltpu

GRID_W = 64
WINDOW = 128
Q_BLOCK = 128
ROPE_THETA = 10000.0
EPS = 1e-6
NEG_INF = -1e30
LOG2_E = 1.4426950408889634

VMEM_LIMIT_BYTES = 56 * 1024 * 1024
MOD_ROWS = 16
PROJ_ROWS = 512
PROJ_SUB_ROWS = 256
DENSE_Q_ROWS = 128

bf16 = jnp.bfloat16
f32 = jnp.float32

_NT = (((1,), (1,)), ((), ()))


def _params(n_axes):
    return pltpu.CompilerParams(
        dimension_semantics=("arbitrary",) * n_axes, vmem_limit_bytes=VMEM_LIMIT_BYTES)


def _mod_kernel(cc_ref, w_ref, b_ref, o_ref):
    cc = cc_ref[...]
    a = (cc * jax.nn.sigmoid(cc)).astype(bf16)
    o_ref[...] = jnp.dot(a, w_ref[...].astype(bf16), preferred_element_type=f32) + b_ref[...]


def _modulation(cc, w_mod, b_mod):
    depth, d, n = w_mod.shape
    tn = 1024 if n % 1024 == 0 else n
    return pl.pallas_call(
        _mod_kernel,
        grid=(depth, n // tn),
        in_specs=[
            pl.BlockSpec((MOD_ROWS, d), lambda l, j: (0, 0)),
            pl.BlockSpec((None, d, tn), lambda l, j: (l, 0, j)),
            pl.BlockSpec((None, 1, tn), lambda l, j: (l, 0, j)),
        ],
        out_specs=pl.BlockSpec((None, MOD_ROWS, tn), lambda l, j: (l, 0, j)),
        out_shape=jax.ShapeDtypeStruct((depth, MOD_ROWS, n), f32),
        compiler_params=_params(2),
        name="modulation",
    )(cc, w_mod, b_mod.reshape(depth, 1, n))


def _head_rms(t, g):
    ms = jnp.mean(t * t, axis=-1, keepdims=True)
    return t * lax.rsqrt(ms + EPS) * g


def _in_proj_kernel(*refs, branch, kv_w, dh, chunk, sub, rope, cache_out):
    x_ref, shift_ref, scale_ref, ng_ref, w_ref, qg_ref, kg_ref = refs[:7]
    refs = refs[7:]
    if rope:
        cos_ref, sin_ref = refs[:2]
        refs = refs[2:]
    q_ref, sz_ref, k_ref, v_ref = refs[:4]
    if cache_out:
        kf_ref, vf_ref = refs[4:6]

    qg = qg_ref[...]
    kg = kg_ref[...]
    q_scale = LOG2_E * dh ** -0.5
    heads_per_chunk = chunk // dh
    for r0 in range(0, x_ref.shape[0], sub):
        rows = slice(r0, r0 + sub)
        x = x_ref[rows, :]
        ms = jnp.mean(x * x, axis=-1, keepdims=True)
        h = x * lax.rsqrt(ms + EPS) * ng_ref[...]
        h = h * (1.0 + scale_ref[...]) + shift_ref[...]
        hb = h.astype(bf16)

        if rope:
            cos = cos_ref[rows, :]
            sin = sin_ref[rows, :]
            lane = lax.broadcasted_iota(jnp.int32, cos.shape, 1)
            first_half = (lane // (dh // 4)) % 2 == 0

            def rotary(t):
                partner = jnp.where(first_half, pltpu.roll(t, dh - dh // 4, 1), pltpu.roll(t, dh // 4, 1))
                return t * cos + partner * sin
        else:
            def rotary(t):
                return t

        for c in range((2 * branch + 2 * kv_w) // chunk):
            col = c * chunk
            r = jnp.dot(hb, w_ref[:, col:col + chunk], preferred_element_type=f32)
            if col < branch:
                for j in range(heads_per_chunk):
                    t = rotary(_head_rms(r[:, j * dh:(j + 1) * dh], qg)) * q_scale
                    q_ref[rows, col + j * dh:col + (j + 1) * dh] = t.astype(bf16)
            elif col < 2 * branch:
                sz_ref[rows, col - branch:col - branch + chunk] = (r * jax.nn.sigmoid(r)).astype(bf16)
            elif col < 2 * branch + kv_w:
                base = col - 2 * branch
                for j in range(heads_per_chunk):
                    t = _head_rms(r[:, j * dh:(j + 1) * dh], kg)
                    if cache_out:
                        kf_ref[rows, base // dh + j, :] = t
                    k_ref[rows, base + j * dh:base + (j + 1) * dh] = rotary(t).astype(bf16)
            else:
                base = col - 2 * branch - kv_w
                if cache_out:
                    for j in range(heads_per_chunk):
                        vf_ref[rows, base // dh + j, :] = r[:, j * dh:(j + 1) * dh]
                v_ref[rows, base:base + chunk] = r.astype(bf16)


def _in_proj(x, mod4, norm_g3, w_in, qg3, kg3, layer, *, rows_per_mod, mod_row0, branch, kv_w, dh,
             rope_tables=None, cache_out=False):
    t, d = x.shape
    tm = PROJ_ROWS
    n_kv = kv_w // dh
    proj_w = w_in.shape[-1]
    chunk = min(512, kv_w)
    rope = rope_tables is not None

    def mod_row(i):
        return mod_row0 + (i * tm) // rows_per_mod if rows_per_mod else mod_row0

    in_specs = [
        pl.BlockSpec((tm, d), lambda i: (i, 0)),
        pl.BlockSpec((None, None, 1, d), lambda i: (layer, mod_row(i), 0, 0)),
        pl.BlockSpec((None, None, 1, d), lambda i: (layer, mod_row(i), 0, 1)),
        pl.BlockSpec((None, 1, d), lambda i: (layer, 0, 0)),
        pl.BlockSpec((None, d, proj_w), lambda i: (layer, 0, 0), pipeline_mode=pl.Buffered(1)),
        pl.BlockSpec((None, 1, dh), lambda i: (layer, 0, 0)),
        pl.BlockSpec((None, 1, dh), lambda i: (layer, 0, 0)),
    ]
    args = [x, mod4, mod4, norm_g3, w_in, qg3, kg3]
    if rope:
        seq = rope_tables[0].shape[0]
        nb = seq // tm
        in_specs += [pl.BlockSpec((tm, dh), lambda i: (i % nb, 0))] * 2
        args += list(rope_tables)
    out_shape = [jax.ShapeDtypeStruct((t, branch), bf16), jax.ShapeDtypeStruct((t, branch), bf16),
                 jax.ShapeDtypeStruct((t, kv_w), bf16), jax.ShapeDtypeStruct((t, kv_w), bf16)]
    out_specs = [pl.BlockSpec((tm, branch), lambda i: (i, 0)), pl.BlockSpec((tm, branch), lambda i: (i, 0)),
                 pl.BlockSpec((tm, kv_w), lambda i: (i, 0)), pl.BlockSpec((tm, kv_w), lambda i: (i, 0))]
    if cache_out:
        out_shape += [jax.ShapeDtypeStruct((t, n_kv, dh), f32)] * 2
        out_specs += [pl.BlockSpec((tm, n_kv, dh), lambda i: (i, 0, 0))] * 2
    kern = functools.partial(_in_proj_kernel, branch=branch, kv_w=kv_w, dh=dh, chunk=chunk,
                             sub=min(PROJ_SUB_ROWS, tm), rope=rope, cache_out=cache_out)
    return pl.pallas_call(
        kern, grid=(t // tm,), in_specs=in_specs, out_specs=out_specs, out_shape=out_shape,
        compiler_params=_params(1), name="in_proj_latent" if rope else "in_proj_context",
    )(*args)


def _stack_groups(qb, groups, dh):
    return jnp.concatenate([qb[:, g * dh:(g + 1) * dh] for g in range(groups)], axis=0)


def _sink_column(sink_ref, head0, groups, rows):
    rowg = lax.broadcasted_iota(jnp.int32, (groups * rows, 1), 0) // rows
    col = jnp.zeros((groups * rows, 1), f32)
    for g in range(groups):
        col = jnp.where(rowg == g, sink_ref[head0 + g], col)
    return col * LOG2_E


def _store_gated(o_ref, row0, rows, col0, o, szb, groups, dh):
    for g in range(groups):
        u = o[g * rows:(g + 1) * rows] * szb[:, g * dh:(g + 1) * dh].astype(f32)
        o_ref[pl.ds(row0, rows), col0 + g * dh:col0 + (g + 1) * dh] = u.astype(bf16)


def _pipeline_blocks(seq, tq, scores, row_max, attend):
    n = seq // tq
    assert n % 2 == 0 and n >= 2
    last = seq - tq

    def start(q):
        return pl.multiple_of(jnp.minimum(q, last), tq)

    scores(0, 0)
    row_max(0)
    scores(tq, 1)

    def body(j, carry):
        q0 = pl.multiple_of(2 * j * tq, tq)
        attend(q0, 0)
        row_max(1)
        scores(start(q0 + 2 * tq), 0)
        attend(q0 + tq, 1)
        row_max(0)
        scores(start(q0 + 3 * tq), 1)
        return carry

    lax.fori_loop(0, n // 2, body, 0)


def _attn_context_kernel(*refs, groups, dh, use_sink):
    if use_sink:
        sink_ref, q_ref, sz_ref, k_ref, v_ref, o_ref = refs
    else:
        q_ref, sz_ref, k_ref, v_ref, o_ref = refs
    rows = q_ref.shape[0]
    gw = groups * dh
    ones = jnp.ones((rows, dh), bf16)
    for h in range(k_ref.shape[1] // dh):
        qs = _stack_groups(q_ref[:, h * gw:(h + 1) * gw], groups, dh)
        s = lax.dot_general(qs, k_ref[:, h * dh:(h + 1) * dh], _NT, preferred_element_type=f32)
        m = jnp.max(s, axis=-1, keepdims=True)
        if use_sink:
            sink = _sink_column(sink_ref, h * groups, groups, rows)
            m = jnp.maximum(m, sink)
        p = jnp.exp2(s - m)
        v_aug = jnp.concatenate([v_ref[:, h * dh:(h + 1) * dh], ones], axis=1)
        oa = jnp.dot(p.astype(bf16), v_aug, preferred_element_type=f32)
        den = oa[:, dh:]
        if use_sink:
            den = den + jnp.exp2(sink - m)
        _store_gated(o_ref, 0, rows, h * gw, oa[:, :dh] / den, sz_ref[:, h * gw:(h + 1) * gw], groups, dh)


def _attn_context(q, sz, k, v, sink, *, batch, seq, groups, dh):
    t, branch = q.shape
    kv_w = k.shape[1]
    use_sink = sink is not None
    in_specs = [
        pl.BlockSpec((seq, branch), lambda b: (b, 0)),
        pl.BlockSpec((seq, branch), lambda b: (b, 0)),
        pl.BlockSpec((seq, kv_w), lambda b: (b, 0)),
        pl.BlockSpec((seq, kv_w), lambda b: (b, 0)),
    ]
    args = [q, sz, k, v]
    if use_sink:
        in_specs = [pl.BlockSpec(memory_space=pltpu.SMEM)] + in_specs
        args = [sink] + args
    return pl.pallas_call(
        functools.partial(_attn_context_kernel, groups=groups, dh=dh, use_sink=use_sink),
        grid=(batch,), in_specs=in_specs,
        out_specs=pl.BlockSpec((seq, branch), lambda b: (b, 0)),
        out_shape=jax.ShapeDtypeStruct((t, branch), bf16),
        compiler_params=_params(1), name="attn_context",
    )(*args)


def _attn_window_kernel(sink_ref, q_ref, sz_ref, k_ref, v_ref, kc_ref, vc_ref, o_ref,
                        kcb_ref, vca_ref, va_ref, s_ref, m_ref, *, groups, dh, tq, window):
    seq = q_ref.shape[0]
    past = kc_ref.shape[0]
    span = tq + 2 * window
    rows = groups * tq
    kcb_ref[...] = kc_ref[...].astype(bf16)
    vca_ref[:, 0:dh] = vc_ref[...].astype(bf16)
    vca_ref[:, dh:] = jnp.ones((past, dh), bf16)
    va_ref[:, 0:dh] = v_ref[...]
    va_ref[:, dh:] = jnp.ones((seq, dh), bf16)
    sink = _sink_column(sink_ref, pl.program_id(1) * groups, groups, tq)
    row_tok = lax.broadcasted_iota(jnp.int32, (rows, span), 0) % tq
    rel = row_tok - lax.broadcasted_iota(jnp.int32, (rows, span), 1)

    def key_start(q0):
        return pl.multiple_of(jnp.clip(q0 - window, 0, seq - span), tq)

    def scores(q0, slot):
        k0 = key_start(q0)
        qs = _stack_groups(q_ref[pl.ds(q0, tq), :], groups, dh)
        s_ctx = lax.dot_general(qs, kcb_ref[...], _NT, preferred_element_type=f32)
        s_win = lax.dot_general(qs, k_ref[pl.ds(k0, span), :], _NT, preferred_element_type=f32)
        dist = rel + (q0 - k0)
        s_win = jnp.where((dist >= -window) & (dist <= window), s_win, NEG_INF)
        s_ref[slot] = jnp.concatenate([s_ctx, s_win], axis=1)

    def row_max(slot):
        m_ref[slot] = jnp.maximum(jnp.max(s_ref[slot], axis=-1, keepdims=True), sink)

    def attend(q0, slot):
        k0 = key_start(q0)
        m = m_ref[slot]
        p = jnp.exp2(s_ref[slot] - m).astype(bf16)
        oa = (jnp.dot(p[:, :past], vca_ref[...], preferred_element_type=f32)
              + jnp.dot(p[:, past:], va_ref[pl.ds(k0, span), :], preferred_element_type=f32))
        o = oa[:, :dh] / (oa[:, dh:] + jnp.exp2(sink - m))
        _store_gated(o_ref, q0, tq, 0, o, sz_ref[pl.ds(q0, tq), :], groups, dh)

    _pipeline_blocks(seq, tq, scores, row_max, attend)


def _attn_window(q, sz, k, v, cache_k4, cache_v4, sink, layer, *, batch, seq, groups, dh):
    t, branch = q.shape
    n_kv = k.shape[1] // dh
    gw = groups * dh
    past = cache_k4.shape[2]
    span = Q_BLOCK + 2 * WINDOW
    assert seq >= span and seq % Q_BLOCK == 0 and WINDOW % Q_BLOCK == 0
    ctx_spec = pl.BlockSpec((None, None, past, dh), lambda b, h: (b, layer, 0, h))
    return pl.pallas_call(
        functools.partial(_attn_window_kernel, groups=groups, dh=dh, tq=Q_BLOCK, window=WINDOW),
        grid=(batch, n_kv),
        in_specs=[
            pl.BlockSpec(memory_space=pltpu.SMEM),
            pl.BlockSpec((seq, gw), lambda b, h: (b, h)),
            pl.BlockSpec((seq, gw), lambda b, h: (b, h)),
            pl.BlockSpec((seq, dh), lambda b, h: (b, h)),
            pl.BlockSpec((seq, dh), lambda b, h: (b, h)),
            ctx_spec, ctx_spec,
        ],
        out_specs=pl.BlockSpec((seq, gw), lambda b, h: (b, h)),
        out_shape=jax.ShapeDtypeStruct((t, branch), bf16),
        scratch_shapes=[pltpu.VMEM((past, dh), bf16), pltpu.VMEM((past, 2 * dh), bf16),
                        pltpu.VMEM((seq, 2 * dh), bf16),
                        pltpu.VMEM((2, groups * Q_BLOCK, past + span), f32),
                        pltpu.VMEM((2, groups * Q_BLOCK, 1), f32)],
        compiler_params=_params(2), name="attn_window",
    )(sink, q, sz, k, v, cache_k4, cache_v4)


def _attn_dense_kernel(q_ref, sz_ref, k_ref, v_ref, kc_ref, vc_ref, o_ref, kall_ref, vall_ref, s_ref, m_ref, *,
                       groups, dh, tq):
    seq = k_ref.shape[0]
    kall_ref[0:seq, :] = k_ref[...]
    kall_ref[seq:, :] = kc_ref[...].astype(bf16)
    vall_ref[0:seq, 0:dh] = v_ref[...]
    vall_ref[seq:, 0:dh] = vc_ref[...].astype(bf16)
    vall_ref[:, dh:] = jnp.ones((vall_ref.shape[0], dh), bf16)

    def scores(q0, slot):
        qs = _stack_groups(q_ref[pl.ds(q0, tq), :], groups, dh)
        s_ref[slot] = lax.dot_general(qs, kall_ref[...], _NT, preferred_element_type=f32)

    def row_max(slot):
        m_ref[slot] = jnp.max(s_ref[slot], axis=-1, keepdims=True)

    def attend(q0, slot):
        p = jnp.exp2(s_ref[slot] - m_ref[slot])
        oa = jnp.dot(p.astype(bf16), vall_ref[...], preferred_element_type=f32)
        o = oa[:, :dh] / oa[:, dh:]
        _store_gated(o_ref, q0, tq, 0, o, sz_ref[pl.ds(q0, tq), :], groups, dh)

    _pipeline_blocks(seq, tq, scores, row_max, attend)


def _attn_dense(q, sz, k, v, cache_k4, cache_v4, layer, *, batch, seq, groups, dh):
    t, branch = q.shape
    n_kv = k.shape[1] // dh
    gw = groups * dh
    past = cache_k4.shape[2]
    tq = DENSE_Q_ROWS
    ctx_spec = pl.BlockSpec((None, None, past, dh), lambda b, h: (b, layer, 0, h))
    return pl.pallas_call(
        functools.partial(_attn_dense_kernel, groups=groups, dh=dh, tq=tq),
        grid=(batch, n_kv),
        in_specs=[
            pl.BlockSpec((seq, gw), lambda b, h: (b, h)),
            pl.BlockSpec((seq, gw), lambda b, h: (b, h)),
            pl.BlockSpec((seq, dh), lambda b, h: (b, h)),
            pl.BlockSpec((seq, dh), lambda b, h: (b, h)),
            ctx_spec, ctx_spec,
        ],
        out_specs=pl.BlockSpec((seq, gw), lambda b, h: (b, h)),
        out_shape=jax.ShapeDtypeStruct((t, branch), bf16),
        scratch_shapes=[pltpu.VMEM((seq + past, dh), bf16), pltpu.VMEM((seq + past, 2 * dh), bf16),
                        pltpu.VMEM((2, groups * tq, seq + past), f32), pltpu.VMEM((2, groups * tq, 1), f32)],
        compiler_params=_params(2), name="attn_dense",
    )(q, sz, k, v, cache_k4, cache_v4)


def _out_proj_kernel(u_ref, x_ref, gate_ref, w_ref, o_ref):
    y = jnp.dot(u_ref[...], w_ref[...], preferred_element_type=f32)
    o_ref[...] = x_ref[...] + gate_ref[...] * y


def _out_proj(u, x, mod4, w_out, layer, *, rows_per_mod, mod_row0):
    t, d = x.shape
    tm = PROJ_ROWS
    branch = u.shape[1]

    def mod_row(i):
        return mod_row0 + (i * tm) // rows_per_mod if rows_per_mod else mod_row0

    return pl.pallas_call(
        _out_proj_kernel, grid=(t // tm,),
        in_specs=[
            pl.BlockSpec((tm, branch), lambda i: (i, 0)),
            pl.BlockSpec((tm, d), lambda i: (i, 0)),
            pl.BlockSpec((None, None, 1, d), lambda i: (layer, mod_row(i), 0, 2)),
            pl.BlockSpec((None, branch, d), lambda i: (layer, 0, 0), pipeline_mode=pl.Buffered(1)),
        ],
        out_specs=pl.BlockSpec((tm, d), lambda i: (i, 0)),
        out_shape=jax.ShapeDtypeStruct((t, d), f32),
        compiler_params=_params(1), name="out_proj",
    )(u, x, mod4, w_out)


def _rope_tables(n_tokens, dh):
    rows = n_tokens // GRID_W
    row = jnp.repeat(jnp.arange(rows), GRID_W).astype(f32)
    col = jnp.tile(jnp.arange(GRID_W), rows).astype(f32)
    n_freq = dh // 4
    inv = ROPE_THETA ** (-jnp.arange(n_freq, dtype=f32) / n_freq)
    ar, ac = row[:, None] * inv, col[:, None] * inv
    cos = jnp.concatenate([jnp.cos(ar), jnp.cos(ar), jnp.cos(ac), jnp.cos(ac)], axis=1)
    sin = jnp.concatenate([-jnp.sin(ar), jnp.sin(ar), -jnp.sin(ac), jnp.sin(ac)], axis=1)
    return cos, sin


def kernel(x_prompt, x_sample, c, cache_k, cache_v, c_ctx, norm_g, w_mod, b_mod, w_in,
           q_norm_g, k_norm_g, sinks, w_out):
    batch, seq, d = x_prompt.shape
    dec_batch, dec_seq, _ = x_sample.shape
    depth = w_in.shape[0]
    past, n_kv, dh = cache_k.shape[2:]
    n_heads = sinks.shape[1]
    groups = n_heads // n_kv
    branch = n_heads * dh
    kv_w = n_kv * dh
    assert dec_batch + 1 <= MOD_ROWS

    cc = jnp.concatenate([c_ctx[None], c, jnp.zeros((MOD_ROWS - 1 - dec_batch, d), f32)], axis=0)
    mod4 = _modulation(cc, w_mod, b_mod).reshape(depth, MOD_ROWS, 1, 3 * d)

    w_in_b = w_in.astype(bf16)
    w_out_b = w_out.astype(bf16)
    norm_g3 = norm_g.reshape(depth, 1, d)
    qg3 = q_norm_g.reshape(depth, 1, dh)
    kg3 = k_norm_g.reshape(depth, 1, dh)
    cache_k4 = cache_k.reshape(dec_batch, depth, past, kv_w)
    cache_v4 = cache_v.reshape(dec_batch, depth, past, kv_w)
    rope = _rope_tables(dec_seq, dh)

    xp = x_prompt.reshape(batch * seq, d)
    xs = x_sample.reshape(dec_batch * dec_seq, d)
    dims = dict(branch=branch, kv_w=kv_w, dh=dh)
    new_k, new_v = [], []
    for layer in range(depth):
        use_window = layer % 2 == 0
        sink = sinks[layer // 2] if use_window else None

        q, sz, k, v, kf, vf = _in_proj(xp, mod4, norm_g3, w_in_b, qg3, kg3, layer, rows_per_mod=0,
                                       mod_row0=0, cache_out=True, **dims)
        u = _attn_context(q, sz, k, v, sink, batch=batch, seq=seq, groups=groups, dh=dh)
        xp = _out_proj(u, xp, mod4, w_out_b, layer, rows_per_mod=0, mod_row0=0)
        new_k.append(kf.reshape(batch, seq, n_kv, dh))
        new_v.append(vf.reshape(batch, seq, n_kv, dh))

        q, sz, k, v = _in_proj(xs, mod4, norm_g3, w_in_b, qg3, kg3, layer, rows_per_mod=dec_seq,
                               mod_row0=1, rope_tables=rope, **dims)
        if use_window:
            u = _attn_window(q, sz, k, v, cache_k4, cache_v4, sink, layer, batch=dec_batch, seq=dec_seq,
                             groups=groups, dh=dh)
        else:
            u = _attn_dense(q, sz, k, v, cache_k4, cache_v4, layer, batch=dec_batch, seq=dec_seq,
                            groups=groups, dh=dh)
        xs = _out_proj(u, xs, mod4, w_out_b, layer, rows_per_mod=dec_seq, mod_row0=1)

    return (xp.reshape(batch, seq, d), xs.reshape(dec_batch, dec_seq, d),
            jnp.stack(new_k, axis=1), jnp.stack(new_v, axis=1))
```

```python
import functools

import jax
import jax.numpy as jnp
from jax import lax
from jax.experimental import pallas as pl
from jax.experimental.pallas import tpu as pltpu

GRID_W = 64
WINDOW = 128
Q_BLOCK = 128
ROPE_THETA = 10000.0
EPS = 1e-6
NEG_INF = -1e30
LOG2_E = 1.4426950408889634

VMEM_LIMIT_BYTES = 56 * 1024 * 1024
MOD_ROWS = 16
PROJ_ROWS = 512
PROJ_SUB_ROWS = 256
WINDOW_PIPE_PAIRS = 2
DENSE_PIPE_PAIRS = 4
DENSE_Q_ROWS = 64

bf16 = jnp.bfloat16
f32 = jnp.float32

_NT = (((1,), (1,)), ((), ()))


def _params(n_axes):
    return pltpu.CompilerParams(
        dimension_semantics=("arbitrary",) * n_axes, vmem_limit_bytes=VMEM_LIMIT_BYTES)


def _mod_kernel(cc_ref, w_ref, b_ref, o_ref):
    cc = cc_ref[...]
    a = (cc * jax.nn.sigmoid(cc)).astype(bf16)
    o_ref[...] = jnp.dot(a, w_ref[...].astype(bf16), preferred_element_type=f32) + b_ref[...]


def _modulation(cc, w_mod, b_mod):
    depth, d, n = w_mod.shape
    tn = 1024 if n % 1024 == 0 else n
    return pl.pallas_call(
        _mod_kernel,
        grid=(depth, n // tn),
        in_specs=[
            pl.BlockSpec((MOD_ROWS, d), lambda l, j: (0, 0)),
            pl.BlockSpec((None, d, tn), lambda l, j: (l, 0, j)),
            pl.BlockSpec((None, 1, tn), lambda l, j: (l, 0, j)),
        ],
        out_specs=pl.BlockSpec((None, MOD_ROWS, tn), lambda l, j: (l, 0, j)),
        out_shape=jax.ShapeDtypeStruct((depth, MOD_ROWS, n), f32),
        compiler_params=_params(2),
        name="modulation",
    )(cc, w_mod, b_mod.reshape(depth, 1, n))


def _head_rms(t, g):
    ms = jnp.mean(t * t, axis=-1, keepdims=True)
    return t * lax.rsqrt(ms + EPS) * g


def _in_proj_kernel(*refs, branch, kv_w, dh, chunk, sub, rope, cache_out):
    x_ref, shift_ref, scale_ref, ng_ref, w_ref, qg_ref, kg_ref = refs[:7]
    refs = refs[7:]
    if rope:
        cos_ref, sin_ref = refs[:2]
        refs = refs[2:]
    q_ref, sz_ref, k_ref, v_ref = refs[:4]
    if cache_out:
        kf_ref, vf_ref = refs[4:6]

    qg = qg_ref[...]
    kg = kg_ref[...]
    q_scale = LOG2_E * dh ** -0.5
    heads_per_chunk = chunk // dh
    for r0 in range(0, x_ref.shape[0], sub):
        rows = slice(r0, r0 + sub)
        x = x_ref[rows, :]
        ms = jnp.mean(x * x, axis=-1, keepdims=True)
        h = x * lax.rsqrt(ms + EPS) * ng_ref[...]
        h = h * (1.0 + scale_ref[...]) + shift_ref[...]
        hb = h.astype(bf16)

        if rope:
            cos = cos_ref[rows, :]
            sin = sin_ref[rows, :]
            lane = lax.broadcasted_iota(jnp.int32, cos.shape, 1)
            first_half = (lane // (dh // 4)) % 2 == 0

            def rotary(t):
                partner = jnp.where(first_half, pltpu.roll(t, dh - dh // 4, 1), pltpu.roll(t, dh // 4, 1))
                return t * cos + partner * sin
        else:
            def rotary(t):
                return t

        for c in range((2 * branch + 2 * kv_w) // chunk):
            col = c * chunk
            r = jnp.dot(hb, w_ref[:, col:col + chunk], preferred_element_type=f32)
            if col < branch:
                for j in range(heads_per_chunk):
                    t = rotary(_head_rms(r[:, j * dh:(j + 1) * dh], qg)) * q_scale
                    q_ref[rows, col + j * dh:col + (j + 1) * dh] = t.astype(bf16)
            elif col < 2 * branch:
                sz_ref[rows, col - branch:col - branch + chunk] = (r * jax.nn.sigmoid(r)).astype(bf16)
            elif col < 2 * branch + kv_w:
                base = col - 2 * branch
                for j in range(heads_per_chunk):
                    t = _head_rms(r[:, j * dh:(j + 1) * dh], kg)
                    if cache_out:
                        kf_ref[rows, base // dh + j, :] = t
                    k_ref[rows, base + j * dh:base + (j + 1) * dh] = rotary(t).astype(bf16)
            else:
                base = col - 2 * branch - kv_w
                if cache_out:
                    for j in range(heads_per_chunk):
                        vf_ref[rows, base // dh + j, :] = r[:, j * dh:(j + 1) * dh]
                v_ref[rows, base:base + chunk] = r.astype(bf16)


def _in_proj(x, mod4, norm_g3, w_in, qg3, kg3, layer, *, rows_per_mod, mod_row0, branch, kv_w, dh,
             rope_tables=None, cache_out=False):
    t, d = x.shape
    tm = PROJ_ROWS
    n_kv = kv_w // dh
    proj_w = w_in.shape[-1]
    chunk = min(512, kv_w)
    rope = rope_tables is not None

    def mod_row(i):
        return mod_row0 + (i * tm) // rows_per_mod if rows_per_mod else mod_row0

    in_specs = [
        pl.BlockSpec((tm, d), lambda i: (i, 0)),
        pl.BlockSpec((None, None, 1, d), lambda i: (layer, mod_row(i), 0, 0)),
        pl.BlockSpec((None, None, 1, d), lambda i: (layer, mod_row(i), 0, 1)),
        pl.BlockSpec((None, 1, d), lambda i: (layer, 0, 0)),
        pl.BlockSpec((None, d, proj_w), lambda i: (layer, 0, 0), pipeline_mode=pl.Buffered(1)),
        pl.BlockSpec((None, 1, dh), lambda i: (layer, 0, 0)),
        pl.BlockSpec((None, 1, dh), lambda i: (layer, 0, 0)),
    ]
    args = [x, mod4, mod4, norm_g3, w_in, qg3, kg3]
    if rope:
        seq = rope_tables[0].shape[0]
        nb = seq // tm
        in_specs += [pl.BlockSpec((tm, dh), lambda i: (i % nb, 0))] * 2
        args += list(rope_tables)
    out_shape = [jax.ShapeDtypeStruct((t, branch), bf16), jax.ShapeDtypeStruct((t, branch), bf16),
                 jax.ShapeDtypeStruct((t, kv_w), bf16), jax.ShapeDtypeStruct((t, kv_w), bf16)]
    out_specs = [pl.BlockSpec((tm, branch), lambda i: (i, 0)), pl.BlockSpec((tm, branch), lambda i: (i, 0)),
                 pl.BlockSpec((tm, kv_w), lambda i: (i, 0)), pl.BlockSpec((tm, kv_w), lambda i: (i, 0))]
    if cache_out:
        out_shape += [jax.ShapeDtypeStruct((t, n_kv, dh), f32)] * 2
        out_specs += [pl.BlockSpec((tm, n_kv, dh), lambda i: (i, 0, 0))] * 2
    kern = functools.partial(_in_proj_kernel, branch=branch, kv_w=kv_w, dh=dh, chunk=chunk,
                             sub=min(PROJ_SUB_ROWS, tm), rope=rope, cache_out=cache_out)
    return pl.pallas_call(
        kern, grid=(t // tm,), in_specs=in_specs, out_specs=out_specs, out_shape=out_shape,
        compiler_params=_params(1), name="in_proj_latent" if rope else "in_proj_context",
    )(*args)


def _stack_groups(qb, groups, dh):
    return jnp.concatenate([qb[:, g * dh:(g + 1) * dh] for g in range(groups)], axis=0)


def _sink_column(sink_ref, head0, groups, rows):
    rowg = lax.broadcasted_iota(jnp.int32, (groups * rows, 1), 0) // rows
    col = jnp.zeros((groups * rows, 1), f32)
    for g in range(groups):
        col = jnp.where(rowg == g, sink_ref[head0 + g], col)
    return col * LOG2_E


def _store_gated(o_ref, row0, rows, col0, o, szb, groups, dh):
    for g in range(groups):
        u = o[g * rows:(g + 1) * rows] * szb[:, g * dh:(g + 1) * dh].astype(f32)
        o_ref[pl.ds(row0, rows), col0 + g * dh:col0 + (g + 1) * dh] = u.astype(bf16)


def _pipeline_blocks(seq, tq, scores, row_max, attend, pairs):
    n = seq // tq
    per_trip = 2 * pairs
    assert n % per_trip == 0
    last = seq - tq

    def start(q):
        return pl.multiple_of(jnp.minimum(q, last), tq)

    scores(0, 0)
    row_max(0)
    scores(tq, 1)

    def body(j, carry):
        for b in range(per_trip):
            slot = b % 2
            q0 = pl.multiple_of((j * per_trip + b) * tq, tq)
            attend(q0, slot)
            row_max(1 - slot)
            scores(start(q0 + 2 * tq), slot)
        return carry

    lax.fori_loop(0, n // per_trip, body, 0)


def _attn_context_kernel(*refs, groups, dh, use_sink):
    if use_sink:
        sink_ref, q_ref, sz_ref, k_ref, v_ref, o_ref = refs
    else:
        q_ref, sz_ref, k_ref, v_ref, o_ref = refs
    rows = q_ref.shape[0]
    gw = groups * dh
    ones = jnp.ones((rows, dh), bf16)
    for h in range(k_ref.shape[1] // dh):
        qs = _stack_groups(q_ref[:, h * gw:(h + 1) * gw], groups, dh)
        s = lax.dot_general(qs, k_ref[:, h * dh:(h + 1) * dh], _NT, preferred_element_type=f32)
        m = jnp.max(s, axis=-1, keepdims=True)
        if use_sink:
            sink = _sink_column(sink_ref, h * groups, groups, rows)
            m = jnp.maximum(m, sink)
        p = jnp.exp2(s - m)
        v_aug = jnp.concatenate([v_ref[:, h * dh:(h + 1) * dh], ones], axis=1)
        oa = jnp.dot(p.astype(bf16), v_aug, preferred_element_type=f32)
        den = oa[:, dh:]
        if use_sink:
            den = den + jnp.exp2(sink - m)
        _store_gated(o_ref, 0, rows, h * gw, oa[:, :dh] / den, sz_ref[:, h * gw:(h + 1) * gw], groups, dh)


def _attn_context(q, sz, k, v, sink, *, batch, seq, groups, dh):
    t, branch = q.shape
    kv_w = k.shape[1]
    use_sink = sink is not None
    in_specs = [
        pl.BlockSpec((seq, branch), lambda b: (b, 0)),
        pl.BlockSpec((seq, branch), lambda b: (b, 0)),
        pl.BlockSpec((seq, kv_w), lambda b: (b, 0)),
        pl.BlockSpec((seq, kv_w), lambda b: (b, 0)),
    ]
    args = [q, sz, k, v]
    if use_sink:
        in_specs = [pl.BlockSpec(memory_space=pltpu.SMEM)] + in_specs
        args = [sink] + args
    return pl.pallas_call(
        functools.partial(_attn_context_kernel, groups=groups, dh=dh, use_sink=use_sink),
        grid=(batch,), in_specs=in_specs,
        out_specs=pl.BlockSpec((seq, branch), lambda b: (b, 0)),
        out_shape=jax.ShapeDtypeStruct((t, branch), bf16),
        compiler_params=_params(1), name="attn_context",
    )(*args)


def _attn_window_kernel(sink_ref, q_ref, sz_ref, k_ref, v_ref, kc_ref, vc_ref, o_ref,
                        kcb_ref, vca_ref, va_ref, s_ref, m_ref, *, groups, dh, tq, window):
    seq = q_ref.shape[0]
    past = kc_ref.shape[0]
    span = tq + 2 * window
    rows = groups * tq
    kcb_ref[...] = kc_ref[...].astype(bf16)
    vca_ref[:, 0:dh] = vc_ref[...].astype(bf16)
    vca_ref[:, dh:] = jnp.ones((past, dh), bf16)
    va_ref[:, 0:dh] = v_ref[...]
    va_ref[:, dh:] = jnp.ones((seq, dh), bf16)
    sink = _sink_column(sink_ref, pl.program_id(1) * groups, groups, tq)
    row_tok = lax.broadcasted_iota(jnp.int32, (rows, span), 0) % tq
    rel = row_tok - lax.broadcasted_iota(jnp.int32, (rows, span), 1)

    def key_start(q0):
        return pl.multiple_of(jnp.clip(q0 - window, 0, seq - span), tq)

    def scores(q0, slot):
        k0 = key_start(q0)
        qs = _stack_groups(q_ref[pl.ds(q0, tq), :], groups, dh)
        s_ctx = lax.dot_general(qs, kcb_ref[...], _NT, preferred_element_type=f32)
        s_win = lax.dot_general(qs, k_ref[pl.ds(k0, span), :], _NT, preferred_element_type=f32)
        dist = rel + (q0 - k0)
        s_win = jnp.where((dist >= -window) & (dist <= window), s_win, NEG_INF)
        s_ref[slot] = jnp.concatenate([s_ctx, s_win], axis=1)

    def row_max(slot):
        m_ref[slot] = jnp.maximum(jnp.max(s_ref[slot], axis=-1, keepdims=True), sink)

    def attend(q0, slot):
        k0 = key_start(q0)
        m = m_ref[slot]
        p = jnp.exp2(s_ref[slot] - m).astype(bf16)
        oa = (jnp.dot(p[:, :past], vca_ref[...], preferred_element_type=f32)
              + jnp.dot(p[:, past:], va_ref[pl.ds(k0, span), :], preferred_element_type=f32))
        o = oa[:, :dh] / (oa[:, dh:] + jnp.exp2(sink - m))
        _store_gated(o_ref, q0, tq, 0, o, sz_ref[pl.ds(q0, tq), :], groups, dh)

    _pipeline_blocks(seq, tq, scores, row_max, attend, WINDOW_PIPE_PAIRS)


def _attn_window(q, sz, k, v, cache_k4, cache_v4, sink, layer, *, batch, seq, groups, dh):
    t, branch = q.shape
    n_kv = k.shape[1] // dh
    gw = groups * dh
    past = cache_k4.shape[2]
    span = Q_BLOCK + 2 * WINDOW
    assert seq >= span and seq % Q_BLOCK == 0 and WINDOW % Q_BLOCK == 0
    ctx_spec = pl.BlockSpec((None, None, past, dh), lambda b, h: (b, layer, 0, h))
    return pl.pallas_call(
        functools.partial(_attn_window_kernel, groups=groups, dh=dh, tq=Q_BLOCK, window=WINDOW),
        grid=(batch, n_kv),
        in_specs=[
            pl.BlockSpec(memory_space=pltpu.SMEM),
            pl.BlockSpec((seq, gw), lambda b, h: (b, h)),
            pl.BlockSpec((seq, gw), lambda b, h: (b, h)),
            pl.BlockSpec((seq, dh), lambda b, h: (b, h)),
            pl.BlockSpec((seq, dh), lambda b, h: (b, h)),
            ctx_spec, ctx_spec,
        ],
        out_specs=pl.BlockSpec((seq, gw), lambda b, h: (b, h)),
        out_shape=jax.ShapeDtypeStruct((t, branch), bf16),
        scratch_shapes=[pltpu.VMEM((past, dh), bf16), pltpu.VMEM((past, 2 * dh), bf16),
                        pltpu.VMEM((seq, 2 * dh), bf16),
                        pltpu.VMEM((2, groups * Q_BLOCK, past + span), f32),
                        pltpu.VMEM((2, groups * Q_BLOCK, 1), f32)],
        compiler_params=_params(2), name="attn_window",
    )(sink, q, sz, k, v, cache_k4, cache_v4)


def _attn_dense_kernel(q_ref, sz_ref, k_ref, v_ref, kc_ref, vc_ref, o_ref, kall_ref, vall_ref, s_ref, m_ref, *,
                       groups, dh, tq):
    seq = k_ref.shape[0]
    kall_ref[0:seq, :] = k_ref[...]
    kall_ref[seq:, :] = kc_ref[...].astype(bf16)
    vall_ref[0:seq, 0:dh] = v_ref[...]
    vall_ref[seq:, 0:dh] = vc_ref[...].astype(bf16)
    vall_ref[:, dh:] = jnp.ones((vall_ref.shape[0], dh), bf16)

    def scores(q0, slot):
        qs = _stack_groups(q_ref[pl.ds(q0, tq), :], groups, dh)
        s_ref[slot] = lax.dot_general(qs, kall_ref[...], _NT, preferred_element_type=f32)

    def row_max(slot):
        m_ref[slot] = jnp.max(s_ref[slot], axis=-1, keepdims=True)

    def attend(q0, slot):
        p = jnp.exp2(s_ref[slot] - m_ref[slot])
        oa = jnp.dot(p.astype(bf16), vall_ref[...], preferred_element_type=f32)
        o = oa[:, :dh] / oa[:, dh:]
        _store_gated(o_ref, q0, tq, 0, o, sz_ref[pl.ds(q0, tq), :], groups, dh)

    _pipeline_blocks(seq, tq, scores, row_max, attend, DENSE_PIPE_PAIRS)


def _attn_dense(q, sz, k, v, cache_k4, cache_v4, layer, *, batch, seq, groups, dh):
    t, branch = q.shape
    n_kv = k.shape[1] // dh
    gw = groups * dh
    past = cache_k4.shape[2]
    tq = DENSE_Q_ROWS
    ctx_spec = pl.BlockSpec((None, None, past, dh), lambda b, h: (b, layer, 0, h))
    return pl.pallas_call(
        functools.partial(_attn_dense_kernel, groups=groups, dh=dh, tq=tq),
        grid=(batch, n_kv),
        in_specs=[
            pl.BlockSpec((seq, gw), lambda b, h: (b, h)),
            pl.BlockSpec((seq, gw), lambda b, h: (b, h)),
            pl.BlockSpec((seq, dh), lambda b, h: (b, h)),
            pl.BlockSpec((seq, dh), lambda b, h: (b, h)),
            ctx_spec, ctx_spec,
        ],
        out_specs=pl.BlockSpec((seq, gw), lambda b, h: (b, h)),
        out_shape=jax.ShapeDtypeStruct((t, branch), bf16),
        scratch_shapes=[pltpu.VMEM((seq + past, dh), bf16), pltpu.VMEM((seq + past, 2 * dh), bf16),
                        pltpu.VMEM((2, groups * tq, seq + past), f32), pltpu.VMEM((2, groups * tq, 1), f32)],
        compiler_params=_params(2), name="attn_dense",
    )(q, sz, k, v, cache_k4, cache_v4)


def _out_proj_kernel(u_ref, x_ref, gate_ref, w_ref, o_ref):
    y = jnp.dot(u_ref[...], w_ref[...], preferred_element_type=f32)
    o_ref[...] = x_ref[...] + gate_ref[...] * y


def _out_proj(u, x, mod4, w_out, layer, *, rows_per_mod, mod_row0):
    t, d = x.shape
    tm = PROJ_ROWS
    branch = u.shape[1]

    def mod_row(i):
        return mod_row0 + (i * tm) // rows_per_mod if rows_per_mod else mod_row0

    return pl.pallas_call(
        _out_proj_kernel, grid=(t // tm,),
        in_specs=[
            pl.BlockSpec((tm, branch), lambda i: (i, 0)),
            pl.BlockSpec((tm, d), lambda i: (i, 0)),
            pl.BlockSpec((None, None, 1, d), lambda i: (layer, mod_row(i), 0, 2)),
            pl.BlockSpec((None, branch, d), lambda i: (layer, 0, 0), pipeline_mode=pl.Buffered(1)),
        ],
        out_specs=pl.BlockSpec((tm, d), lambda i: (i, 0)),
        out_shape=jax.ShapeDtypeStruct((t, d), f32),
        compiler_params=_params(1), name="out_proj",
    )(u, x, mod4, w_out)


def _rope_tables(n_tokens, dh):
    rows = n_tokens // GRID_W
    row = jnp.repeat(jnp.arange(rows), GRID_W).astype(f32)
    col = jnp.tile(jnp.arange(GRID_W), rows).astype(f32)
    n_freq = dh // 4
    inv = ROPE_THETA ** (-jnp.arange(n_freq, dtype=f32) / n_freq)
    ar, ac = row[:, None] * inv, col[:, None] * inv
    cos = jnp.concatenate([jnp.cos(ar), jnp.cos(ar), jnp.cos(ac), jnp.cos(ac)], axis=1)
    sin = jnp.concatenate([-jnp.sin(ar), jnp.sin(ar), -jnp.sin(ac), jnp.sin(ac)], axis=1)
    return cos, sin


def kernel(x_prompt, x_sample, c, cache_k, cache_v, c_ctx, norm_g, w_mod, b_mod, w_in,
           q_norm_g, k_norm_g, sinks, w_out):
    batch, seq, d = x_prompt.shape
    dec_batch, dec_seq, _ = x_sample.shape
    depth = w_in.shape[0]
    past, n_kv, dh = cache_k.shape[2:]
    n_heads = sinks.shape[1]
    groups = n_heads // n_kv
    branch = n_heads * dh
    kv_w = n_kv * dh
    assert dec_batch + 1 <= MOD_ROWS

    cc = jnp.concatenate([c_ctx[None], c, jnp.zeros((MOD_ROWS - 1 - dec_batch, d), f32)], axis=0)
    mod4 = _modulation(cc, w_mod, b_mod).reshape(depth, MOD_ROWS, 1, 3 * d)

    w_in_b = w_in.astype(bf16)
    w_out_b = w_out.astype(bf16)
    norm_g3 = norm_g.reshape(depth, 1, d)
    qg3 = q_norm_g.reshape(depth, 1, dh)
    kg3 = k_norm_g.reshape(depth, 1, dh)
    cache_k4 = cache_k.reshape(dec_batch, depth, past, kv_w)
    cache_v4 = cache_v.reshape(dec_batch, depth, past, kv_w)
    rope = _rope_tables(dec_seq, dh)

    xp = x_prompt.reshape(batch * seq, d)
    xs = x_sample.reshape(dec_batch * dec_seq, d)
    dims = dict(branch=branch, kv_w=kv_w, dh=dh)
    new_k, new_v = [], []
    for layer in range(depth):
        use_window = layer % 2 == 0
        sink = sinks[layer // 2] if use_window else None

        q, sz, k, v, kf, vf = _in_proj(xp, mod4, norm_g3, w_in_b, qg3, kg3, layer, rows_per_mod=0,
                                       mod_row0=0, cache_out=True, **dims)
        u = _attn_context(q, sz, k, v, sink, batch=batch, seq=seq, groups=groups, dh=dh)
        xp = _out_proj(u, xp, mod4, w_out_b, layer, rows_per_mod=0, mod_row0=0)
        new_k.append(kf.reshape(batch, seq, n_kv, dh))
        new_v.append(vf.reshape(batch, seq, n_kv, dh))

        q, sz, k, v = _in_proj(xs, mod4, norm_g3, w_in_b, qg3, kg3, layer, rows_per_mod=dec_seq,
                               mod_row0=1, rope_tables=rope, **dims)
        if use_window:
            u = _attn_window(q, sz, k, v, cache_k4, cache_v4, sink, layer, batch=dec_batch, seq=dec_seq,
                             groups=groups, dh=dh)
        else:
            u = _attn_dense(q, sz, k, v, cache_k4, cache_v4, layer, batch=dec_batch, seq=dec_seq,
                            groups=groups, dh=dh)
        xs = _out_proj(u, xs, mod4, w_out_b, layer, rows_per_mod=dec_seq, mod_row0=1)

    return (xp.reshape(batch, seq, d), xs.reshape(dec_batch, dec_seq, d),
            jnp.stack(new_k, axis=1), jnp.stack(new_v, axis=1))
```

```python
import functools

import jax
import jax.numpy as jnp
from jax import lax
from jax.experimental import pallas as pl
from jax.experimental.pallas import tpu as pltpu

GRID_W = 64
WINDOW = 128
Q_BLOCK = 128
ROPE_THETA = 10000.0
EPS = 1e-6
NEG_INF = -1e30
LOG2_E = 1.4426950408889634

VMEM_LIMIT_BYTES = 56 * 1024 * 1024
LANES = 128
MOD_ROWS = 16
PROJ_ROWS = 512
PROJ_SUB_ROWS = 256
WINDOW_PIPE_PAIRS = 2
DENSE_PIPE_PAIRS = 4
DENSE_Q_ROWS = 64

bf16 = jnp.bfloat16
f32 = jnp.float32

_NT = (((1,), (1,)), ((), ()))


def _params(n_axes):
    return pltpu.CompilerParams(
        dimension_semantics=("arbitrary",) * n_axes, vmem_limit_bytes=VMEM_LIMIT_BYTES)


def _mod_kernel(cc_ref, w_ref, b_ref, o_ref):
    cc = cc_ref[...]
    a = (cc * jax.nn.sigmoid(cc)).astype(bf16)
    o_ref[...] = jnp.dot(a, w_ref[...].astype(bf16), preferred_element_type=f32) + b_ref[...]


def _modulation(cc, w_mod, b_mod):
    depth, d, n = w_mod.shape
    tn = 1024 if n % 1024 == 0 else n
    return pl.pallas_call(
        _mod_kernel,
        grid=(depth, n // tn),
        in_specs=[
            pl.BlockSpec((MOD_ROWS, d), lambda l, j: (0, 0)),
            pl.BlockSpec((None, d, tn), lambda l, j: (l, 0, j)),
            pl.BlockSpec((None, 1, tn), lambda l, j: (l, 0, j)),
        ],
        out_specs=pl.BlockSpec((None, MOD_ROWS, tn), lambda l, j: (l, 0, j)),
        out_shape=jax.ShapeDtypeStruct((depth, MOD_ROWS, n), f32),
        compiler_params=_params(2),
        name="modulation",
    )(cc, w_mod, b_mod.reshape(depth, 1, n))


def _head_rms(t, g):
    ms = jnp.mean(t * t, axis=-1, keepdims=True)
    return t * lax.rsqrt(ms + EPS) * g


def _in_proj_kernel(*refs, branch, kv_w, dh, chunk, sub, rope, cache_out):
    x_ref, shift_ref, scale_ref, ng_ref, w_ref, qg_ref, kg_ref = refs[:7]
    refs = refs[7:]
    if rope:
        cos_ref, sin_ref = refs[:2]
        refs = refs[2:]
    q_ref, sz_ref, k_ref, v_ref = refs[:4]
    if cache_out:
        kf_ref, vf_ref = refs[4:6]

    qg = qg_ref[...]
    kg = kg_ref[...]
    q_scale = LOG2_E * dh ** -0.5
    heads_per_chunk = chunk // dh
    for r0 in range(0, x_ref.shape[0], sub):
        rows = slice(r0, r0 + sub)
        x = x_ref[rows, :]
        ms = jnp.mean(x * x, axis=-1, keepdims=True)
        h = x * lax.rsqrt(ms + EPS) * ng_ref[...]
        h = h * (1.0 + scale_ref[...]) + shift_ref[...]
        hb = h.astype(bf16)

        if rope:
            cos = cos_ref[rows, :]
            sin = sin_ref[rows, :]
            lane = lax.broadcasted_iota(jnp.int32, cos.shape, 1)
            first_half = (lane // (dh // 4)) % 2 == 0

            def rotary(t):
                partner = jnp.where(first_half, pltpu.roll(t, dh - dh // 4, 1), pltpu.roll(t, dh // 4, 1))
                return t * cos + partner * sin
        else:
            def rotary(t):
                return t

        for c in range((2 * branch + 2 * kv_w) // chunk):
            col = c * chunk
            r = jnp.dot(hb, w_ref[:, col:col + chunk], preferred_element_type=f32)
            if col < branch:
                for j in range(heads_per_chunk):
                    t = rotary(_head_rms(r[:, j * dh:(j + 1) * dh], qg)) * q_scale
                    q_ref[rows, col + j * dh:col + (j + 1) * dh] = t.astype(bf16)
            elif col < 2 * branch:
                sz_ref[rows, col - branch:col - branch + chunk] = (r * jax.nn.sigmoid(r)).astype(bf16)
            elif col < 2 * branch + kv_w:
                base = col - 2 * branch
                for j in range(heads_per_chunk):
                    t = _head_rms(r[:, j * dh:(j + 1) * dh], kg)
                    if cache_out:
                        kf_ref[rows, base // dh + j, :] = t
                    k_ref[rows, base + j * dh:base + (j + 1) * dh] = rotary(t).astype(bf16)
            else:
                base = col - 2 * branch - kv_w
                if cache_out:
                    for j in range(heads_per_chunk):
                        vf_ref[rows, base // dh + j, :] = r[:, j * dh:(j + 1) * dh]
                v_ref[rows, base:base + chunk] = r.astype(bf16)


def _in_proj(x, mod4, norm_g3, w_in, qg3, kg3, layer, *, rows_per_mod, mod_row0, branch, kv_w, dh,
             rope_tables=None, cache_out=False):
    t, d = x.shape
    tm = PROJ_ROWS
    n_kv = kv_w // dh
    proj_w = w_in.shape[-1]
    chunk = min(512, kv_w)
    rope = rope_tables is not None

    def mod_row(i):
        return mod_row0 + (i * tm) // rows_per_mod if rows_per_mod else mod_row0

    in_specs = [
        pl.BlockSpec((tm, d), lambda i: (i, 0)),
        pl.BlockSpec((None, None, 1, d), lambda i: (layer, mod_row(i), 0, 0)),
        pl.BlockSpec((None, None, 1, d), lambda i: (layer, mod_row(i), 0, 1)),
        pl.BlockSpec((None, 1, d), lambda i: (layer, 0, 0)),
        pl.BlockSpec((None, d, proj_w), lambda i: (layer, 0, 0), pipeline_mode=pl.Buffered(1)),
        pl.BlockSpec((None, 1, dh), lambda i: (layer, 0, 0)),
        pl.BlockSpec((None, 1, dh), lambda i: (layer, 0, 0)),
    ]
    args = [x, mod4, mod4, norm_g3, w_in, qg3, kg3]
    if rope:
        seq = rope_tables[0].shape[0]
        nb = seq // tm
        in_specs += [pl.BlockSpec((tm, dh), lambda i: (i % nb, 0))] * 2
        args += list(rope_tables)
    out_shape = [jax.ShapeDtypeStruct((t, branch), bf16), jax.ShapeDtypeStruct((t, branch), bf16),
                 jax.ShapeDtypeStruct((t, kv_w), bf16), jax.ShapeDtypeStruct((t, kv_w), bf16)]
    out_specs = [pl.BlockSpec((tm, branch), lambda i: (i, 0)), pl.BlockSpec((tm, branch), lambda i: (i, 0)),
                 pl.BlockSpec((tm, kv_w), lambda i: (i, 0)), pl.BlockSpec((tm, kv_w), lambda i: (i, 0))]
    if cache_out:
        out_shape += [jax.ShapeDtypeStruct((t, n_kv, dh), f32)] * 2
        out_specs += [pl.BlockSpec((tm, n_kv, dh), lambda i: (i, 0, 0))] * 2
    kern = functools.partial(_in_proj_kernel, branch=branch, kv_w=kv_w, dh=dh, chunk=chunk,
                             sub=min(PROJ_SUB_ROWS, tm), rope=rope, cache_out=cache_out)
    return pl.pallas_call(
        kern, grid=(t // tm,), in_specs=in_specs, out_specs=out_specs, out_shape=out_shape,
        compiler_params=_params(1), name="in_proj_latent" if rope else "in_proj_context",
    )(*args)


def _lane_block_max(s):
    return functools.reduce(jnp.maximum, [s[:, j:j + LANES] for j in range(0, s.shape[1], LANES)])


def _stack_groups(qb, groups, dh):
    return jnp.concatenate([qb[:, g * dh:(g + 1) * dh] for g in range(groups)], axis=0)


def _sink_column(sink_ref, head0, groups, rows):
    rowg = lax.broadcasted_iota(jnp.int32, (groups * rows, 1), 0) // rows
    col = jnp.zeros((groups * rows, 1), f32)
    for g in range(groups):
        col = jnp.where(rowg == g, sink_ref[head0 + g], col)
    return col * LOG2_E


def _store_gated(o_ref, row0, rows, col0, o, szb, groups, dh):
    for g in range(groups):
        u = o[g * rows:(g + 1) * rows] * szb[:, g * dh:(g + 1) * dh].astype(f32)
        o_ref[pl.ds(row0, rows), col0 + g * dh:col0 + (g + 1) * dh] = u.astype(bf16)


def _pipeline_blocks(seq, tq, scores, row_max, attend, pairs):
    n = seq // tq
    per_trip = 2 * pairs
    assert n % per_trip == 0
    last = seq - tq

    def start(q):
        return pl.multiple_of(jnp.minimum(q, last), tq)

    scores(0, 0)
    row_max(0)
    scores(tq, 1)

    def body(j, carry):
        for b in range(per_trip):
            slot = b % 2
            q0 = pl.multiple_of((j * per_trip + b) * tq, tq)
            attend(q0, slot)
            row_max(1 - slot)
            scores(start(q0 + 2 * tq), slot)
        return carry

    lax.fori_loop(0, n // per_trip, body, 0)


def _attn_context_kernel(*refs, groups, dh, use_sink):
    if use_sink:
        sink_ref, q_ref, sz_ref, k_ref, v_ref, o_ref = refs
    else:
        q_ref, sz_ref, k_ref, v_ref, o_ref = refs
    rows = q_ref.shape[0]
    gw = groups * dh
    ones = jnp.ones((rows, dh), bf16)
    for h in range(k_ref.shape[1] // dh):
        qs = _stack_groups(q_ref[:, h * gw:(h + 1) * gw], groups, dh)
        s = lax.dot_general(qs, k_ref[:, h * dh:(h + 1) * dh], _NT, preferred_element_type=f32)
        m = jnp.max(s, axis=-1, keepdims=True)
        if use_sink:
            sink = _sink_column(sink_ref, h * groups, groups, rows)
            m = jnp.maximum(m, sink)
        p = jnp.exp2(s - m)
        v_aug = jnp.concatenate([v_ref[:, h * dh:(h + 1) * dh], ones], axis=1)
        oa = jnp.dot(p.astype(bf16), v_aug, preferred_element_type=f32)
        den = oa[:, dh:]
        if use_sink:
            den = den + jnp.exp2(sink - m)
        _store_gated(o_ref, 0, rows, h * gw, oa[:, :dh] / den, sz_ref[:, h * gw:(h + 1) * gw], groups, dh)


def _attn_context(q, sz, k, v, sink, *, batch, seq, groups, dh):
    t, branch = q.shape
    kv_w = k.shape[1]
    use_sink = sink is not None
    in_specs = [
        pl.BlockSpec((seq, branch), lambda b: (b, 0)),
        pl.BlockSpec((seq, branch), lambda b: (b, 0)),
        pl.BlockSpec((seq, kv_w), lambda b: (b, 0)),
        pl.BlockSpec((seq, kv_w), lambda b: (b, 0)),
    ]
    args = [q, sz, k, v]
    if use_sink:
        in_specs = [pl.BlockSpec(memory_space=pltpu.SMEM)] + in_specs
        args = [sink] + args
    return pl.pallas_call(
        functools.partial(_attn_context_kernel, groups=groups, dh=dh, use_sink=use_sink),
        grid=(batch,), in_specs=in_specs,
        out_specs=pl.BlockSpec((seq, branch), lambda b: (b, 0)),
        out_shape=jax.ShapeDtypeStruct((t, branch), bf16),
        compiler_params=_params(1), name="attn_context",
    )(*args)


def _attn_window_kernel(sink_ref, q_ref, sz_ref, k_ref, v_ref, kc_ref, vc_ref, o_ref,
                        kcb_ref, vca_ref, va_ref, s_ref, m_ref, *, groups, dh, tq, window):
    seq = q_ref.shape[0]
    past = kc_ref.shape[0]
    span = tq + 2 * window
    rows = groups * tq
    kcb_ref[...] = kc_ref[...].astype(bf16)
    vca_ref[:, 0:dh] = vc_ref[...].astype(bf16)
    vca_ref[:, dh:] = jnp.ones((past, dh), bf16)
    va_ref[:, 0:dh] = v_ref[...]
    va_ref[:, dh:] = jnp.ones((seq, dh), bf16)
    sink = _sink_column(sink_ref, pl.program_id(1) * groups, groups, tq)
    row_tok = lax.broadcasted_iota(jnp.int32, (rows, span), 0) % tq
    rel = row_tok - lax.broadcasted_iota(jnp.int32, (rows, span), 1)

    def key_start(q0):
        return pl.multiple_of(jnp.clip(q0 - window, 0, seq - span), tq)

    def scores(q0, slot):
        k0 = key_start(q0)
        qs = _stack_groups(q_ref[pl.ds(q0, tq), :], groups, dh)
        s_ctx = lax.dot_general(qs, kcb_ref[...], _NT, preferred_element_type=f32)
        s_win = lax.dot_general(qs, k_ref[pl.ds(k0, span), :], _NT, preferred_element_type=f32)
        dist = rel + (q0 - k0)
        s_win = jnp.where((dist >= -window) & (dist <= window), s_win, NEG_INF)
        s_ref[slot] = jnp.concatenate([s_ctx, s_win], axis=1)

    def row_max(slot):
        m_ref[slot] = jnp.maximum(jnp.max(s_ref[slot], axis=-1, keepdims=True), sink)

    def attend(q0, slot):
        k0 = key_start(q0)
        m = m_ref[slot]
        p = jnp.exp2(s_ref[slot] - m).astype(bf16)
        oa = (jnp.dot(p[:, :past], vca_ref[...], preferred_element_type=f32)
              + jnp.dot(p[:, past:], va_ref[pl.ds(k0, span), :], preferred_element_type=f32))
        o = oa[:, :dh] / (oa[:, dh:] + jnp.exp2(sink - m))
        _store_gated(o_ref, q0, tq, 0, o, sz_ref[pl.ds(q0, tq), :], groups, dh)

    _pipeline_blocks(seq, tq, scores, row_max, attend, WINDOW_PIPE_PAIRS)


def _attn_window(q, sz, k, v, cache_k4, cache_v4, sink, layer, *, batch, seq, groups, dh):
    t, branch = q.shape
    n_kv = k.shape[1] // dh
    gw = groups * dh
    past = cache_k4.shape[2]
    span = Q_BLOCK + 2 * WINDOW
    assert seq >= span and seq % Q_BLOCK == 0 and WINDOW % Q_BLOCK == 0
    ctx_spec = pl.BlockSpec((None, None, past, dh), lambda b, h: (b, layer, 0, h))
    return pl.pallas_call(
        functools.partial(_attn_window_kernel, groups=groups, dh=dh, tq=Q_BLOCK, window=WINDOW),
        grid=(batch, n_kv),
        in_specs=[
            pl.BlockSpec(memory_space=pltpu.SMEM),
            pl.BlockSpec((seq, gw), lambda b, h: (b, h)),
            pl.BlockSpec((seq, gw), lambda b, h: (b, h)),
            pl.BlockSpec((seq, dh), lambda b, h: (b, h)),
            pl.BlockSpec((seq, dh), lambda b, h: (b, h)),
            ctx_spec, ctx_spec,
        ],
        out_specs=pl.BlockSpec((seq, gw), lambda b, h: (b, h)),
        out_shape=jax.ShapeDtypeStruct((t, branch), bf16),
        scratch_shapes=[pltpu.VMEM((past, dh), bf16), pltpu.VMEM((past, 2 * dh), bf16),
                        pltpu.VMEM((seq, 2 * dh), bf16),
                        pltpu.VMEM((2, groups * Q_BLOCK, past + span), f32),
                        pltpu.VMEM((2, groups * Q_BLOCK, 1), f32)],
        compiler_params=_params(2), name="attn_window",
    )(sink, q, sz, k, v, cache_k4, cache_v4)


def _attn_dense_kernel(q_ref, sz_ref, k_ref, v_ref, kc_ref, vc_ref, o_ref, kall_ref, vall_ref, s_ref, pm_ref, m_ref, *,
                       groups, dh, tq):
    seq = k_ref.shape[0]
    kall_ref[0:seq, :] = k_ref[...]
    kall_ref[seq:, :] = kc_ref[...].astype(bf16)
    vall_ref[0:seq, 0:dh] = v_ref[...]
    vall_ref[seq:, 0:dh] = vc_ref[...].astype(bf16)
    vall_ref[:, dh:] = jnp.ones((vall_ref.shape[0], dh), bf16)

    def scores(q0, slot):
        qs = _stack_groups(q_ref[pl.ds(q0, tq), :], groups, dh)
        s = lax.dot_general(qs, kall_ref[...], _NT, preferred_element_type=f32)
        s_ref[slot] = s
        pm_ref[slot] = _lane_block_max(s)

    def row_max(slot):
        m_ref[slot] = jnp.max(pm_ref[slot], axis=-1, keepdims=True)

    def attend(q0, slot):
        p = jnp.exp2(s_ref[slot] - m_ref[slot])
        oa = jnp.dot(p.astype(bf16), vall_ref[...], preferred_element_type=f32)
        o = oa[:, :dh] / oa[:, dh:]
        _store_gated(o_ref, q0, tq, 0, o, sz_ref[pl.ds(q0, tq), :], groups, dh)

    _pipeline_blocks(seq, tq, scores, row_max, attend, DENSE_PIPE_PAIRS)


def _attn_dense(q, sz, k, v, cache_k4, cache_v4, layer, *, batch, seq, groups, dh):
    t, branch = q.shape
    n_kv = k.shape[1] // dh
    gw = groups * dh
    past = cache_k4.shape[2]
    tq = DENSE_Q_ROWS
    ctx_spec = pl.BlockSpec((None, None, past, dh), lambda b, h: (b, layer, 0, h))
    return pl.pallas_call(
        functools.partial(_attn_dense_kernel, groups=groups, dh=dh, tq=tq),
        grid=(batch, n_kv),
        in_specs=[
            pl.BlockSpec((seq, gw), lambda b, h: (b, h)),
            pl.BlockSpec((seq, gw), lambda b, h: (b, h)),
            pl.BlockSpec((seq, dh), lambda b, h: (b, h)),
            pl.BlockSpec((seq, dh), lambda b, h: (b, h)),
            ctx_spec, ctx_spec,
        ],
        out_specs=pl.BlockSpec((seq, gw), lambda b, h: (b, h)),
        out_shape=jax.ShapeDtypeStruct((t, branch), bf16),
        scratch_shapes=[pltpu.VMEM((seq + past, dh), bf16), pltpu.VMEM((seq + past, 2 * dh), bf16),
                        pltpu.VMEM((2, groups * tq, seq + past), f32),
                        pltpu.VMEM((2, groups * tq, LANES), f32), pltpu.VMEM((2, groups * tq, 1), f32)],
        compiler_params=_params(2), name="attn_dense",
    )(q, sz, k, v, cache_k4, cache_v4)


def _out_proj_kernel(u_ref, x_ref, gate_ref, w_ref, o_ref):
    y = jnp.dot(u_ref[...], w_ref[...], preferred_element_type=f32)
    o_ref[...] = x_ref[...] + gate_ref[...] * y


def _out_proj(u, x, mod4, w_out, layer, *, rows_per_mod, mod_row0):
    t, d = x.shape
    tm = PROJ_ROWS
    branch = u.shape[1]

    def mod_row(i):
        return mod_row0 + (i * tm) // rows_per_mod if rows_per_mod else mod_row0

    return pl.pallas_call(
        _out_proj_kernel, grid=(t // tm,),
        in_specs=[
            pl.BlockSpec((tm, branch), lambda i: (i, 0)),
            pl.BlockSpec((tm, d), lambda i: (i, 0)),
            pl.BlockSpec((None, None, 1, d), lambda i: (layer, mod_row(i), 0, 2)),
            pl.BlockSpec((None, branch, d), lambda i: (layer, 0, 0), pipeline_mode=pl.Buffered(1)),
        ],
        out_specs=pl.BlockSpec((tm, d), lambda i: (i, 0)),
        out_shape=jax.ShapeDtypeStruct((t, d), f32),
        compiler_params=_params(1), name="out_proj",
    )(u, x, mod4, w_out)


def _rope_tables(n_tokens, dh):
    rows = n_tokens // GRID_W
    row = jnp.repeat(jnp.arange(rows), GRID_W).astype(f32)
    col = jnp.tile(jnp.arange(GRID_W), rows).astype(f32)
    n_freq = dh // 4
    inv = ROPE_THETA ** (-jnp.arange(n_freq, dtype=f32) / n_freq)
    ar, ac = row[:, None] * inv, col[:, None] * inv
    cos = jnp.concatenate([jnp.cos(ar), jnp.cos(ar), jnp.cos(ac), jnp.cos(ac)], axis=1)
    sin = jnp.concatenate([-jnp.sin(ar), jnp.sin(ar), -jnp.sin(ac), jnp.sin(ac)], axis=1)
    return cos, sin


def kernel(x_prompt, x_sample, c, cache_k, cache_v, c_ctx, norm_g, w_mod, b_mod, w_in,
           q_norm_g, k_norm_g, sinks, w_out):
    batch, seq, d = x_prompt.shape
    dec_batch, dec_seq, _ = x_sample.shape
    depth = w_in.shape[0]
    past, n_kv, dh = cache_k.shape[2:]
    n_heads = sinks.shape[1]
    groups = n_heads // n_kv
    branch = n_heads * dh
    kv_w = n_kv * dh
    assert dec_batch + 1 <= MOD_ROWS

    cc = jnp.concatenate([c_ctx[None], c, jnp.zeros((MOD_ROWS - 1 - dec_batch, d), f32)], axis=0)
    mod4 = _modulation(cc, w_mod, b_mod).reshape(depth, MOD_ROWS, 1, 3 * d)

    w_in_b = w_in.astype(bf16)
    w_out_b = w_out.astype(bf16)
    norm_g3 = norm_g.reshape(depth, 1, d)
    qg3 = q_norm_g.reshape(depth, 1, dh)
    kg3 = k_norm_g.reshape(depth, 1, dh)
    cache_k4 = cache_k.reshape(dec_batch, depth, past, kv_w)
    cache_v4 = cache_v.reshape(dec_batch, depth, past, kv_w)
    rope = _rope_tables(dec_seq, dh)

    xp = x_prompt.reshape(batch * seq, d)
    xs = x_sample.reshape(dec_batch * dec_seq, d)
    dims = dict(branch=branch, kv_w=kv_w, dh=dh)
    new_k, new_v = [], []
    for layer in range(depth):
        use_window = layer % 2 == 0
        sink = sinks[layer // 2] if use_window else None

        q, sz, k, v, kf, vf = _in_proj(xp, mod4, norm_g3, w_in_b, qg3, kg3, layer, rows_per_mod=0,
                                       mod_row0=0, cache_out=True, **dims)
        u = _attn_context(q, sz, k, v, sink, batch=batch, seq=seq, groups=groups, dh=dh)
        xp = _out_proj(u, xp, mod4, w_out_b, layer, rows_per_mod=0, mod_row0=0)
        new_k.append(kf.reshape(batch, seq, n_kv, dh))
        new_v.append(vf.reshape(batch, seq, n_kv, dh))

        q, sz, k, v = _in_proj(xs, mod4, norm_g3, w_in_b, qg3, kg3, layer, rows_per_mod=dec_seq,
                               mod_row0=1, rope_tables=rope, **dims)
        if use_window:
            u = _attn_window(q, sz, k, v, cache_k4, cache_v4, sink, layer, batch=dec_batch, seq=dec_seq,
                             groups=groups, dh=dh)
        else:
            u = _attn_dense(q, sz, k, v, cache_k4, cache_v4, layer, batch=dec_batch, seq=dec_seq,
                            groups=groups, dh=dh)
        xs = _out_proj(u, xs, mod4, w_out_b, layer, rows_per_mod=dec_seq, mod_row0=1)

    return (xp.reshape(batch, seq, d), xs.reshape(dec_batch, dec_seq, d),
            jnp.stack(new_k, axis=1), jnp.stack(new_v, axis=1))
```

```python
import functools

import jax
import jax.numpy as jnp
from jax import lax
from jax.experimental import pallas as pl
from jax.experimental.pallas import tpu as pltpu

GRID_W = 64
WINDOW = 128
Q_BLOCK = 128
ROPE_THETA = 10000.0
EPS = 1e-6
NEG_INF = -1e30
LOG2_E = 1.4426950408889634

VMEM_LIMIT_BYTES = 56 * 1024 * 1024
LANES = 128
MOD_ROWS = 16
PROJ_ROWS = 512
PROJ_SUB_ROWS = 256
WINDOW_PIPE_PAIRS = 4
DENSE_PIPE_PAIRS = 4
DENSE_Q_ROWS = 64

bf16 = jnp.bfloat16
f32 = jnp.float32

_NT = (((1,), (1,)), ((), ()))


def _params(n_axes):
    return pltpu.CompilerParams(
        dimension_semantics=("arbitrary",) * n_axes, vmem_limit_bytes=VMEM_LIMIT_BYTES)


def _mod_kernel(cc_ref, w_ref, b_ref, o_ref):
    cc = cc_ref[...]
    a = (cc * jax.nn.sigmoid(cc)).astype(bf16)
    o_ref[...] = jnp.dot(a, w_ref[...].astype(bf16), preferred_element_type=f32) + b_ref[...]


def _modulation(cc, w_mod, b_mod):
    depth, d, n = w_mod.shape
    tn = 1024 if n % 1024 == 0 else n
    return pl.pallas_call(
        _mod_kernel,
        grid=(depth, n // tn),
        in_specs=[
            pl.BlockSpec((MOD_ROWS, d), lambda l, j: (0, 0)),
            pl.BlockSpec((None, d, tn), lambda l, j: (l, 0, j)),
            pl.BlockSpec((None, 1, tn), lambda l, j: (l, 0, j)),
        ],
        out_specs=pl.BlockSpec((None, MOD_ROWS, tn), lambda l, j: (l, 0, j)),
        out_shape=jax.ShapeDtypeStruct((depth, MOD_ROWS, n), f32),
        compiler_params=_params(2),
        name="modulation",
    )(cc, w_mod, b_mod.reshape(depth, 1, n))


def _head_rms(t, g):
    ms = jnp.mean(t * t, axis=-1, keepdims=True)
    return t * lax.rsqrt(ms + EPS) * g


def _in_proj_kernel(*refs, branch, kv_w, dh, chunk, sub, rope, cache_out):
    x_ref, shift_ref, scale_ref, ng_ref, w_ref, qg_ref, kg_ref = refs[:7]
    refs = refs[7:]
    if rope:
        cos_ref, sin_ref = refs[:2]
        refs = refs[2:]
    q_ref, sz_ref, k_ref, v_ref = refs[:4]
    if cache_out:
        kf_ref, vf_ref = refs[4:6]

    qg = qg_ref[...]
    kg = kg_ref[...]
    q_scale = LOG2_E * dh ** -0.5
    heads_per_chunk = chunk // dh
    for r0 in range(0, x_ref.shape[0], sub):
        rows = slice(r0, r0 + sub)
        x = x_ref[rows, :]
        ms = jnp.mean(x * x, axis=-1, keepdims=True)
        h = x * lax.rsqrt(ms + EPS) * ng_ref[...]
        h = h * (1.0 + scale_ref[...]) + shift_ref[...]
        hb = h.astype(bf16)

        if rope:
            cos = cos_ref[rows, :]
            sin = sin_ref[rows, :]
            lane = lax.broadcasted_iota(jnp.int32, cos.shape, 1)
            first_half = (lane // (dh // 4)) % 2 == 0

            def rotary(t):
                partner = jnp.where(first_half, pltpu.roll(t, dh - dh // 4, 1), pltpu.roll(t, dh // 4, 1))
                return t * cos + partner * sin
        else:
            def rotary(t):
                return t

        for c in range((2 * branch + 2 * kv_w) // chunk):
            col = c * chunk
            r = jnp.dot(hb, w_ref[:, col:col + chunk], preferred_element_type=f32)
            if col < branch:
                for j in range(heads_per_chunk):
                    t = rotary(_head_rms(r[:, j * dh:(j + 1) * dh], qg)) * q_scale
                    q_ref[rows, col + j * dh:col + (j + 1) * dh] = t.astype(bf16)
            elif col < 2 * branch:
                sz_ref[rows, col - branch:col - branch + chunk] = (r * jax.nn.sigmoid(r)).astype(bf16)
            elif col < 2 * branch + kv_w:
                base = col - 2 * branch
                for j in range(heads_per_chunk):
                    t = _head_rms(r[:, j * dh:(j + 1) * dh], kg)
                    if cache_out:
                        kf_ref[rows, base // dh + j, :] = t
                    k_ref[rows, base + j * dh:base + (j + 1) * dh] = rotary(t).astype(bf16)
            else:
                base = col - 2 * branch - kv_w
                if cache_out:
                    for j in range(heads_per_chunk):
                        vf_ref[rows, base // dh + j, :] = r[:, j * dh:(j + 1) * dh]
                v_ref[rows, base:base + chunk] = r.astype(bf16)


def _in_proj(x, mod4, norm_g3, w_in, qg3, kg3, layer, *, rows_per_mod, mod_row0, branch, kv_w, dh,
             rope_tables=None, cache_out=False):
    t, d = x.shape
    tm = PROJ_ROWS
    n_kv = kv_w // dh
    proj_w = w_in.shape[-1]
    chunk = min(512, kv_w)
    rope = rope_tables is not None

    def mod_row(i):
        return mod_row0 + (i * tm) // rows_per_mod if rows_per_mod else mod_row0

    in_specs = [
        pl.BlockSpec((tm, d), lambda i: (i, 0)),
        pl.BlockSpec((None, None, 1, d), lambda i: (layer, mod_row(i), 0, 0)),
        pl.BlockSpec((None, None, 1, d), lambda i: (layer, mod_row(i), 0, 1)),
        pl.BlockSpec((None, 1, d), lambda i: (layer, 0, 0)),
        pl.BlockSpec((None, d, proj_w), lambda i: (layer, 0, 0), pipeline_mode=pl.Buffered(1)),
        pl.BlockSpec((None, 1, dh), lambda i: (layer, 0, 0)),
        pl.BlockSpec((None, 1, dh), lambda i: (layer, 0, 0)),
    ]
    args = [x, mod4, mod4, norm_g3, w_in, qg3, kg3]
    if rope:
        seq = rope_tables[0].shape[0]
        nb = seq // tm
        in_specs += [pl.BlockSpec((tm, dh), lambda i: (i % nb, 0))] * 2
        args += list(rope_tables)
    out_shape = [jax.ShapeDtypeStruct((t, branch), bf16), jax.ShapeDtypeStruct((t, branch), bf16),
                 jax.ShapeDtypeStruct((t, kv_w), bf16), jax.ShapeDtypeStruct((t, kv_w), bf16)]
    out_specs = [pl.BlockSpec((tm, branch), lambda i: (i, 0)), pl.BlockSpec((tm, branch), lambda i: (i, 0)),
                 pl.BlockSpec((tm, kv_w), lambda i: (i, 0)), pl.BlockSpec((tm, kv_w), lambda i: (i, 0))]
    if cache_out:
        out_shape += [jax.ShapeDtypeStruct((t, n_kv, dh), f32)] * 2
        out_specs += [pl.BlockSpec((tm, n_kv, dh), lambda i: (i, 0, 0))] * 2
    kern = functools.partial(_in_proj_kernel, branch=branch, kv_w=kv_w, dh=dh, chunk=chunk,
                             sub=min(PROJ_SUB_ROWS, tm), rope=rope, cache_out=cache_out)
    return pl.pallas_call(
        kern, grid=(t // tm,), in_specs=in_specs, out_specs=out_specs, out_shape=out_shape,
        compiler_params=_params(1), name="in_proj_latent" if rope else "in_proj_context",
    )(*args)


def _lane_block_max(s):
    return functools.reduce(jnp.maximum, [s[:, j:j + LANES] for j in range(0, s.shape[1], LANES)])


def _stack_groups(qb, groups, dh):
    return jnp.concatenate([qb[:, g * dh:(g + 1) * dh] for g in range(groups)], axis=0)


def _sink_column(sink_ref, head0, groups, rows):
    rowg = lax.broadcasted_iota(jnp.int32, (groups * rows, 1), 0) // rows
    col = jnp.zeros((groups * rows, 1), f32)
    for g in range(groups):
        col = jnp.where(rowg == g, sink_ref[head0 + g], col)
    return col * LOG2_E


def _store_gated(o_ref, row0, rows, col0, o, szb, groups, dh):
    for g in range(groups):
        u = o[g * rows:(g + 1) * rows] * szb[:, g * dh:(g + 1) * dh].astype(f32)
        o_ref[pl.ds(row0, rows), col0 + g * dh:col0 + (g + 1) * dh] = u.astype(bf16)


def _pipeline_blocks(seq, tq, scores, row_max, attend, pairs):
    n = seq // tq
    per_trip = 2 * pairs
    assert n % per_trip == 0
    last = seq - tq

    def start(q):
        return pl.multiple_of(jnp.minimum(q, last), tq)

    scores(0, 0)
    row_max(0)
    scores(tq, 1)

    def body(j, carry):
        for b in range(per_trip):
            slot = b % 2
            q0 = pl.multiple_of((j * per_trip + b) * tq, tq)
            attend(q0, slot)
            row_max(1 - slot)
            scores(start(q0 + 2 * tq), slot)
        return carry

    lax.fori_loop(0, n // per_trip, body, 0)


def _attn_context_kernel(*refs, groups, dh, use_sink):
    if use_sink:
        sink_ref, q_ref, sz_ref, k_ref, v_ref, o_ref = refs
    else:
        q_ref, sz_ref, k_ref, v_ref, o_ref = refs
    rows = q_ref.shape[0]
    gw = groups * dh
    ones = jnp.ones((rows, dh), bf16)
    for h in range(k_ref.shape[1] // dh):
        qs = _stack_groups(q_ref[:, h * gw:(h + 1) * gw], groups, dh)
        s = lax.dot_general(qs, k_ref[:, h * dh:(h + 1) * dh], _NT, preferred_element_type=f32)
        m = jnp.max(s, axis=-1, keepdims=True)
        if use_sink:
            sink = _sink_column(sink_ref, h * groups, groups, rows)
            m = jnp.maximum(m, sink)
        p = jnp.exp2(s - m)
        v_aug = jnp.concatenate([v_ref[:, h * dh:(h + 1) * dh], ones], axis=1)
        oa = jnp.dot(p.astype(bf16), v_aug, preferred_element_type=f32)
        den = oa[:, dh:]
        if use_sink:
            den = den + jnp.exp2(sink - m)
        _store_gated(o_ref, 0, rows, h * gw, oa[:, :dh] / den, sz_ref[:, h * gw:(h + 1) * gw], groups, dh)


def _attn_context(q, sz, k, v, sink, *, batch, seq, groups, dh):
    t, branch = q.shape
    kv_w = k.shape[1]
    use_sink = sink is not None
    in_specs = [
        pl.BlockSpec((seq, branch), lambda b: (b, 0)),
        pl.BlockSpec((seq, branch), lambda b: (b, 0)),
        pl.BlockSpec((seq, kv_w), lambda b: (b, 0)),
        pl.BlockSpec((seq, kv_w), lambda b: (b, 0)),
    ]
    args = [q, sz, k, v]
    if use_sink:
        in_specs = [pl.BlockSpec(memory_space=pltpu.SMEM)] + in_specs
        args = [sink] + args
    return pl.pallas_call(
        functools.partial(_attn_context_kernel, groups=groups, dh=dh, use_sink=use_sink),
        grid=(batch,), in_specs=in_specs,
        out_specs=pl.BlockSpec((seq, branch), lambda b: (b, 0)),
        out_shape=jax.ShapeDtypeStruct((t, branch), bf16),
        compiler_params=_params(1), name="attn_context",
    )(*args)


def _attn_window_kernel(sink_ref, q_ref, sz_ref, k_ref, v_ref, kc_ref, vc_ref, o_ref,
                        kcb_ref, vca_ref, va_ref, s_ref, m_ref, *, groups, dh, tq, window):
    seq = q_ref.shape[0]
    past = kc_ref.shape[0]
    span = tq + 2 * window
    rows = groups * tq
    kcb_ref[...] = kc_ref[...].astype(bf16)
    vca_ref[:, 0:dh] = vc_ref[...].astype(bf16)
    vca_ref[:, dh:] = jnp.ones((past, dh), bf16)
    va_ref[:, 0:dh] = v_ref[...]
    va_ref[:, dh:] = jnp.ones((seq, dh), bf16)
    sink = _sink_column(sink_ref, pl.program_id(1) * groups, groups, tq)
    rel = lax.broadcasted_iota(jnp.int32, (tq, span), 0) - lax.broadcasted_iota(jnp.int32, (tq, span), 1)

    def key_start(q0):
        return pl.multiple_of(jnp.clip(q0 - window, 0, seq - span), tq)

    def scores(q0, slot):
        k0 = key_start(q0)
        qs = _stack_groups(q_ref[pl.ds(q0, tq), :], groups, dh)
        s_ctx = lax.dot_general(qs, kcb_ref[...], _NT, preferred_element_type=f32)
        s_win = lax.dot_general(qs, k_ref[pl.ds(k0, span), :], _NT, preferred_element_type=f32)
        dist = rel + (q0 - k0)
        in_band = (dist >= -window) & (dist <= window)
        s_win = jnp.concatenate(
            [jnp.where(in_band, s_win[g * tq:(g + 1) * tq], NEG_INF) for g in range(groups)], axis=0)
        s_ref[slot] = jnp.concatenate([s_ctx, s_win], axis=1)

    def row_max(slot):
        m_ref[slot] = jnp.maximum(jnp.max(s_ref[slot], axis=-1, keepdims=True), sink)

    def attend(q0, slot):
        k0 = key_start(q0)
        m = m_ref[slot]
        p = jnp.exp2(s_ref[slot] - m).astype(bf16)
        oa = (jnp.dot(p[:, :past], vca_ref[...], preferred_element_type=f32)
              + jnp.dot(p[:, past:], va_ref[pl.ds(k0, span), :], preferred_element_type=f32))
        o = oa[:, :dh] / (oa[:, dh:] + jnp.exp2(sink - m))
        _store_gated(o_ref, q0, tq, 0, o, sz_ref[pl.ds(q0, tq), :], groups, dh)

    _pipeline_blocks(seq, tq, scores, row_max, attend, WINDOW_PIPE_PAIRS)


def _attn_window(q, sz, k, v, cache_k4, cache_v4, sink, layer, *, batch, seq, groups, dh):
    t, branch = q.shape
    n_kv = k.shape[1] // dh
    gw = groups * dh
    past = cache_k4.shape[2]
    span = Q_BLOCK + 2 * WINDOW
    assert seq >= span and seq % Q_BLOCK == 0 and WINDOW % Q_BLOCK == 0
    ctx_spec = pl.BlockSpec((None, None, past, dh), lambda b, h: (b, layer, 0, h))
    return pl.pallas_call(
        functools.partial(_attn_window_kernel, groups=groups, dh=dh, tq=Q_BLOCK, window=WINDOW),
        grid=(batch, n_kv),
        in_specs=[
            pl.BlockSpec(memory_space=pltpu.SMEM),
            pl.BlockSpec((seq, gw), lambda b, h: (b, h)),
            pl.BlockSpec((seq, gw), lambda b, h: (b, h)),
            pl.BlockSpec((seq, dh), lambda b, h: (b, h)),
            pl.BlockSpec((seq, dh), lambda b, h: (b, h)),
            ctx_spec, ctx_spec,
        ],
        out_specs=pl.BlockSpec((seq, gw), lambda b, h: (b, h)),
        out_shape=jax.ShapeDtypeStruct((t, branch), bf16),
        scratch_shapes=[pltpu.VMEM((past, dh), bf16), pltpu.VMEM((past, 2 * dh), bf16),
                        pltpu.VMEM((seq, 2 * dh), bf16),
                        pltpu.VMEM((2, groups * Q_BLOCK, past + span), f32),
                        pltpu.VMEM((2, groups * Q_BLOCK, 1), f32)],
        compiler_params=_params(2), name="attn_window",
    )(sink, q, sz, k, v, cache_k4, cache_v4)


def _attn_dense_kernel(q_ref, sz_ref, k_ref, v_ref, kc_ref, vc_ref, o_ref, kall_ref, vall_ref, s_ref, pm_ref, m_ref, *,
                       groups, dh, tq):
    seq = k_ref.shape[0]
    kall_ref[0:seq, :] = k_ref[...]
    kall_ref[seq:, :] = kc_ref[...].astype(bf16)
    vall_ref[0:seq, 0:dh] = v_ref[...]
    vall_ref[seq:, 0:dh] = vc_ref[...].astype(bf16)
    vall_ref[:, dh:] = jnp.ones((vall_ref.shape[0], dh), bf16)

    def scores(q0, slot):
        qs = _stack_groups(q_ref[pl.ds(q0, tq), :], groups, dh)
        s = lax.dot_general(qs, kall_ref[...], _NT, preferred_element_type=f32)
        s_ref[slot] = s
        pm_ref[slot] = _lane_block_max(s)

    def row_max(slot):
        m_ref[slot] = jnp.max(pm_ref[slot], axis=-1, keepdims=True)

    def attend(q0, slot):
        p = jnp.exp2(s_ref[slot] - m_ref[slot])
        oa = jnp.dot(p.astype(bf16), vall_ref[...], preferred_element_type=f32)
        o = oa[:, :dh] / oa[:, dh:]
        _store_gated(o_ref, q0, tq, 0, o, sz_ref[pl.ds(q0, tq), :], groups, dh)

    _pipeline_blocks(seq, tq, scores, row_max, attend, DENSE_PIPE_PAIRS)


def _attn_dense(q, sz, k, v, cache_k4, cache_v4, layer, *, batch, seq, groups, dh):
    t, branch = q.shape
    n_kv = k.shape[1] // dh
    gw = groups * dh
    past = cache_k4.shape[2]
    tq = DENSE_Q_ROWS
    ctx_spec = pl.BlockSpec((None, None, past, dh), lambda b, h: (b, layer, 0, h))
    return pl.pallas_call(
        functools.partial(_attn_dense_kernel, groups=groups, dh=dh, tq=tq),
        grid=(batch, n_kv),
        in_specs=[
            pl.BlockSpec((seq, gw), lambda b, h: (b, h)),
            pl.BlockSpec((seq, gw), lambda b, h: (b, h)),
            pl.BlockSpec((seq, dh), lambda b, h: (b, h)),
            pl.BlockSpec((seq, dh), lambda b, h: (b, h)),
            ctx_spec, ctx_spec,
        ],
        out_specs=pl.BlockSpec((seq, gw), lambda b, h: (b, h)),
        out_shape=jax.ShapeDtypeStruct((t, branch), bf16),
        scratch_shapes=[pltpu.VMEM((seq + past, dh), bf16), pltpu.VMEM((seq + past, 2 * dh), bf16),
                        pltpu.VMEM((2, groups * tq, seq + past), f32),
                        pltpu.VMEM((2, groups * tq, LANES), f32), pltpu.VMEM((2, groups * tq, 1), f32)],
        compiler_params=_params(2), name="attn_dense",
    )(q, sz, k, v, cache_k4, cache_v4)


def _out_proj_kernel(u_ref, x_ref, gate_ref, w_ref, o_ref):
    y = jnp.dot(u_ref[...], w_ref[...], preferred_element_type=f32)
    o_ref[...] = x_ref[...] + gate_ref[...] * y


def _out_proj(u, x, mod4, w_out, layer, *, rows_per_mod, mod_row0):
    t, d = x.shape
    tm = PROJ_ROWS
    branch = u.shape[1]

    def mod_row(i):
        return mod_row0 + (i * tm) // rows_per_mod if rows_per_mod else mod_row0

    return pl.pallas_call(
        _out_proj_kernel, grid=(t // tm,),
        in_specs=[
            pl.BlockSpec((tm, branch), lambda i: (i, 0)),
            pl.BlockSpec((tm, d), lambda i: (i, 0)),
            pl.BlockSpec((None, None, 1, d), lambda i: (layer, mod_row(i), 0, 2)),
            pl.BlockSpec((None, branch, d), lambda i: (layer, 0, 0), pipeline_mode=pl.Buffered(1)),
        ],
        out_specs=pl.BlockSpec((tm, d), lambda i: (i, 0)),
        out_shape=jax.ShapeDtypeStruct((t, d), f32),
        compiler_params=_params(1), name="out_proj",
    )(u, x, mod4, w_out)


def _rope_tables(n_tokens, dh):
    rows = n_tokens // GRID_W
    row = jnp.repeat(jnp.arange(rows), GRID_W).astype(f32)
    col = jnp.tile(jnp.arange(GRID_W), rows).astype(f32)
    n_freq = dh // 4
    inv = ROPE_THETA ** (-jnp.arange(n_freq, dtype=f32) / n_freq)
    ar, ac = row[:, None] * inv, col[:, None] * inv
    cos = jnp.concatenate([jnp.cos(ar), jnp.cos(ar), jnp.cos(ac), jnp.cos(ac)], axis=1)
    sin = jnp.concatenate([-jnp.sin(ar), jnp.sin(ar), -jnp.sin(ac), jnp.sin(ac)], axis=1)
    return cos, sin


def kernel(x_prompt, x_sample, c, cache_k, cache_v, c_ctx, norm_g, w_mod, b_mod, w_in,
           q_norm_g, k_norm_g, sinks, w_out):
    batch, seq, d = x_prompt.shape
    dec_batch, dec_seq, _ = x_sample.shape
    depth = w_in.shape[0]
    past, n_kv, dh = cache_k.shape[2:]
    n_heads = sinks.shape[1]
    groups = n_heads // n_kv
    branch = n_heads * dh
    kv_w = n_kv * dh
    assert dec_batch + 1 <= MOD_ROWS

    cc = jnp.concatenate([c_ctx[None], c, jnp.zeros((MOD_ROWS - 1 - dec_batch, d), f32)], axis=0)
    mod4 = _modulation(cc, w_mod, b_mod).reshape(depth, MOD_ROWS, 1, 3 * d)

    w_in_b = w_in.astype(bf16)
    w_out_b = w_out.astype(bf16)
    norm_g3 = norm_g.reshape(depth, 1, d)
    qg3 = q_norm_g.reshape(depth, 1, dh)
    kg3 = k_norm_g.reshape(depth, 1, dh)
    cache_k4 = cache_k.reshape(dec_batch, depth, past, kv_w)
    cache_v4 = cache_v.reshape(dec_batch, depth, past, kv_w)
    rope = _rope_tables(dec_seq, dh)

    xp = x_prompt.reshape(batch * seq, d)
    xs = x_sample.reshape(dec_batch * dec_seq, d)
    dims = dict(branch=branch, kv_w=kv_w, dh=dh)
    new_k, new_v = [], []
    for layer in range(depth):
        use_window = layer % 2 == 0
        sink = sinks[layer // 2] if use_window else None

        q, sz, k, v, kf, vf = _in_proj(xp, mod4, norm_g3, w_in_b, qg3, kg3, layer, rows_per_mod=0,
                                       mod_row0=0, cache_out=True, **dims)
        u = _attn_context(q, sz, k, v, sink, batch=batch, seq=seq, groups=groups, dh=dh)
        xp = _out_proj(u, xp, mod4, w_out_b, layer, rows_per_mod=0, mod_row0=0)
        new_k.append(kf.reshape(batch, seq, n_kv, dh))
        new_v.append(vf.reshape(batch, seq, n_kv, dh))

        q, sz, k, v = _in_proj(xs, mod4, norm_g3, w_in_b, qg3, kg3, layer, rows_per_mod=dec_seq,
                               mod_row0=1, rope_tables=rope, **dims)
        if use_window:
            u = _attn_window(q, sz, k, v, cache_k4, cache_v4, sink, layer, batch=dec_batch, seq=dec_seq,
                             groups=groups, dh=dh)
        else:
            u = _attn_dense(q, sz, k, v, cache_k4, cache_v4, layer, batch=dec_batch, seq=dec_seq,
                            groups=groups, dh=dh)
        xs = _out_proj(u, xs, mod4, w_out_b, layer, rows_per_mod=dec_seq, mod_row0=1)

    return (xp.reshape(batch, seq, d), xs.reshape(dec_batch, dec_seq, d),
            jnp.stack(new_k, axis=1), jnp.stack(new_v, axis=1))
```
